```python
import jax, jax.numpy as jnp
from jax import lax
import numpy as np

D_MODEL = 1024
BATCH = 32
SEQ = 2048
DEPTH = 2

MIX_WIDTH = D_MODEL
CONV_CH = MIX_WIDTH // 2
CONV_GROUPS = 8
CONV_WIDTH = 31
RWKV_HEAD = 64
RWKV_CH = MIX_WIDTH - CONV_CH
RWKV_HEADS = RWKV_CH // RWKV_HEAD
LORA_W = 32
LORA_A = 32
LORA_V = 32
LORA_G = 96
RWKV_IN = 3 * RWKV_CH + LORA_W + LORA_A + LORA_G
IN_COLS = 2 * CONV_CH + RWKV_IN
N_GROUPS = 4
EXPERTS_PER_GROUP = 8
N_EXPERTS = N_GROUPS * EXPERTS_PER_GROUP
TOP_K = 2
EXPERT_FF = 512
MOE_BLOCK = 512
PLE_DIM = 256
NORM_EPS = 1e-6
LN_EPS = 1e-5
GN_EPS = 64e-5

kernel_name = "hymba_conformer_rwkv7_hmoe"


def rms_norm(x, g):
    x32 = x.astype(jnp.float32)
    y = x32 * lax.rsqrt(jnp.mean(x32 * x32, axis=-1, keepdims=True) + NORM_EPS)
    return (y * g.astype(jnp.float32)).astype(x.dtype)


def token_shift(z):
    return jnp.pad(z, ((0, 0), (1, 0), (0, 0)))[:, :-1, :]


def conformer_conv(z, conv_w, conv_b, ln_g, ln_b):
    val, gate = jnp.split(z, 2, axis=-1)
    u = val * jax.nn.sigmoid(gate)
    u = lax.conv_general_dilated(
        u, conv_w[:, None, :].astype(u.dtype), window_strides=(1,),
        padding=[(CONV_WIDTH - 1, 0)],
        dimension_numbers=('NWC', 'WIO', 'NWC'),
        feature_group_count=CONV_CH) + conv_b
    u32 = u.astype(jnp.float32)
    mean = jnp.mean(u32, axis=-1, keepdims=True)
    var = jnp.mean(jnp.square(u32 - mean), axis=-1, keepdims=True)
    un = (u32 - mean) * lax.rsqrt(var + LN_EPS) * ln_g.astype(jnp.float32) + ln_b.astype(jnp.float32)
    return jax.nn.silu(un).astype(z.dtype)


def rwkv7_scan(r, w, k, v, a_vec, b_vec):
    Bb, Ss, H, N = r.shape

    def step(state, inp):
        r_t, w_t, k_t, v_t, a_t, b_t = inp
        sa = jnp.einsum('bhvk,bhk->bhv', state, a_t)
        state = (state * w_t[:, :, None, :]
                 + sa[..., None] * b_t[:, :, None, :]
                 + v_t[..., None] * k_t[:, :, None, :])
        y_t = jnp.einsum('bhvk,bhk->bhv', state, r_t)
        return state, y_t

    xs = tuple(jnp.moveaxis(t, 1, 0) for t in (r, w, k, v, a_vec, b_vec))
    s0 = jnp.zeros((Bb, H, N, N), jnp.float32)
    _, ys = lax.scan(step, s0, xs)
    return jnp.moveaxis(ys, 0, 1)


def rwkv7_time_mix(z, v_first, mu, w0, w2, a0, a2, g2, k_k, k_a, r_k, gn_g, gn_b, vres):
    Bb, Ss, _ = z.shape
    f32 = jnp.float32
    zs = z + (token_shift(z) - z) * mu
    r, k, v, w_lo, a_lo, g_lo = jnp.split(
        zs, [RWKV_CH, 2 * RWKV_CH, 3 * RWKV_CH, 3 * RWKV_CH + LORA_W,
             3 * RWKV_CH + LORA_W + LORA_A], axis=-1)
    w = -jax.nn.softplus(-(w0 + jnp.tanh(w_lo) @ w2)) - 0.5
    decay = jnp.exp(-jnp.exp(w.astype(f32)))
    if vres is None:
        v_first = v
    else:
        v0, v1, v2 = vres
        v = v + (v_first - v) * jax.nn.sigmoid(v0 + (v @ v1) @ v2)
    a = jax.nn.sigmoid(a0 + a_lo @ a2)
    g = jax.nn.sigmoid(g_lo) @ g2
    hs = (Bb, Ss, RWKV_HEADS, RWKV_HEAD)
    kk = (k * k_k).astype(f32).reshape(hs)
    kk = kk / jnp.maximum(jnp.sqrt(jnp.sum(kk * kk, axis=-1, keepdims=True)), 1e-12)
    k = k * (1.0 + (a - 1.0) * k_a)
    r_h = r.astype(f32).reshape(hs)
    k_h = k.astype(f32).reshape(hs)
    v_h = v.astype(f32).reshape(hs)
    a_h = a.astype(f32).reshape(hs)
    y = rwkv7_scan(r_h, decay.reshape(hs), k_h, v_h, -kk, kk * a_h)
    mean = jnp.mean(y, axis=-1, keepdims=True)
    var = jnp.mean(jnp.square(y - mean), axis=-1, keepdims=True)
    y = (y - mean) * lax.rsqrt(var + GN_EPS)
    y = y * gn_g.astype(f32).reshape(RWKV_HEADS, RWKV_HEAD) + gn_b.astype(f32).reshape(RWKV_HEADS, RWKV_HEAD)
    y = y + jnp.sum(r_h * k_h * r_k.astype(f32), axis=-1, keepdims=True) * v_h
    y = y.reshape(Bb, Ss, RWKV_CH).astype(z.dtype) * g
    return y, v_first


def hierarchical_moe(u, wg, bg, we, be, w_gate, w_up, w_down):
    Bb, Ss, D = u.shape
    t = u.reshape(-1, D)
    T = t.shape[0]
    glog = (t @ wg).astype(jnp.float32) + bg.astype(jnp.float32)
    gprob = jax.nn.softmax(glog, axis=-1)
    g_idx = jnp.argmax(glog, axis=-1)
    g_p = jnp.take_along_axis(gprob, g_idx[:, None], axis=1)[:, 0]
    elog_all = jnp.einsum('td,gde->tge', t, we).astype(jnp.float32) + be.astype(jnp.float32)
    elog = jnp.take_along_axis(elog_all, g_idx[:, None, None], axis=1)[:, 0]
    top_v, top_i = lax.top_k(elog, TOP_K)
    gate_w = jax.nn.softmax(top_v, axis=-1) * g_p[:, None]
    e_idx = g_idx[:, None].astype(jnp.int32) * EXPERTS_PER_GROUP + top_i.astype(jnp.int32)

    A = T * TOP_K
    n_blocks = -(-A // MOE_BLOCK) + N_EXPERTS
    P = n_blocks * MOE_BLOCK
    flat_e = e_idx.reshape(-1)
    order = jnp.argsort(flat_e)
    se = flat_e[order]
    tok = (order // TOP_K).astype(jnp.int32)
    gate = gate_w.reshape(-1)[order]
    counts = jnp.zeros((N_EXPERTS,), jnp.int32).at[flat_e].add(1)
    starts = jnp.cumsum(counts) - counts
    pcounts = (counts + MOE_BLOCK - 1) // MOE_BLOCK * MOE_BLOCK
    pend = jnp.cumsum(pcounts)
    pstarts = pend - pcounts
    dest = pstarts[se] + (jnp.arange(A, dtype=jnp.int32) - starts[se])
    buf_tok = jnp.full((P,), T, jnp.int32).at[dest].set(tok)
    buf_gate = jnp.zeros((P,), jnp.float32).at[dest].set(gate)
    blk_e = jnp.minimum(
        jnp.searchsorted(pend, jnp.arange(n_blocks, dtype=jnp.int32) * MOE_BLOCK, side='right'),
        N_EXPERTS - 1)
    t_pad = jnp.concatenate([t, jnp.zeros((1, D), t.dtype)], axis=0)
    xbuf = t_pad[buf_tok].reshape(n_blocks, MOE_BLOCK, D)

    def run_block(args):
        xb, e = args
        hid = jax.nn.silu(xb @ w_gate[e]) * (xb @ w_up[e])
        return hid @ w_down[e]

    ybuf = lax.map(run_block, (xbuf, blk_e)).reshape(P, D)
    ybuf = ybuf * buf_gate[:, None].astype(ybuf.dtype)
    out = jnp.zeros((T + 1, D), t.dtype).at[buf_tok].add(ybuf)[:T]
    return out.reshape(Bb, Ss, D)


def setup_inputs(seed: int = 0) -> dict:
    key = jax.random.key(seed)
    ks = iter(jax.random.split(key, 48))

    def nrm(shape, scale):
        return jax.random.normal(next(ks), shape, jnp.float32) * scale

    def gain(shape):
        return 1.0 + nrm(shape, 0.02)

    L = DEPTH
    Lv = DEPTH - 1
    return {
        "x": nrm((BATCH, SEQ, D_MODEL), 1.0),
        "p": nrm((DEPTH, BATCH, SEQ, PLE_DIM), 1.0),
        "mix_norm": gain((L, D_MODEL)),
        "w_in": nrm((L, D_MODEL, IN_COLS), D_MODEL ** -0.5),
        "conv_w": nrm((L, CONV_WIDTH, CONV_CH), CONV_WIDTH ** -0.5),
        "conv_b": nrm((L, CONV_CH), 0.01),
        "conv_ln_g": gain((L, CONV_CH)),
        "conv_ln_b": nrm((L, CONV_CH), 0.01),
        "rwkv_mu": jax.random.uniform(next(ks), (L, RWKV_IN), jnp.float32),
        "rwkv_w0": -1.5 + nrm((L, RWKV_CH), 0.5),
        "rwkv_w2": nrm((L, LORA_W, RWKV_CH), 0.1),
        "rwkv_a0": nrm((L, RWKV_CH), 0.1),
        "rwkv_a2": nrm((L, LORA_A, RWKV_CH), 0.1),
        "rwkv_g2": nrm((L, LORA_G, RWKV_CH), LORA_G ** -0.5),
        "rwkv_kk": 0.85 + nrm((L, RWKV_CH), 0.02),
        "rwkv_ka": 1.0 + nrm((L, RWKV_CH), 0.02),
        "rwkv_rk": nrm((L, RWKV_HEADS, RWKV_HEAD), 0.1),
        "rwkv_gn_g": gain((L, RWKV_CH)),
        "rwkv_gn_b": nrm((L, RWKV_CH), 0.01),
        "rwkv_v0": nrm((Lv, RWKV_CH), 0.1),
        "rwkv_v1": nrm((Lv, RWKV_CH, LORA_V), RWKV_CH ** -0.5),
        "rwkv_v2": nrm((Lv, LORA_V, RWKV_CH), 0.1),
        "w_out": nrm((L, MIX_WIDTH, D_MODEL), MIX_WIDTH ** -0.5),
        "ffn_norm": gain((L, D_MODEL)),
        "router_group_w": nrm((L, D_MODEL, N_GROUPS), D_MODEL ** -0.5),
        "router_group_b": nrm((L, N_GROUPS), 0.01),
        "router_expert_w": nrm((L, N_GROUPS, D_MODEL, EXPERTS_PER_GROUP), D_MODEL ** -0.5),
        "router_expert_b": nrm((L, N_GROUPS, EXPERTS_PER_GROUP), 0.01),
        "expert_w_gate": nrm((L, N_EXPERTS, D_MODEL, EXPERT_FF), D_MODEL ** -0.5),
        "expert_w_up": nrm((L, N_EXPERTS, D_MODEL, EXPERT_FF), D_MODEL ** -0.5),
        "expert_w_down": nrm((L, N_EXPERTS, EXPERT_FF, D_MODEL), EXPERT_FF ** -0.5),
        "ple_norm": gain((L, D_MODEL)),
        "ple_gate_w": nrm((L, D_MODEL, D_MODEL), D_MODEL ** -0.5),
        "ple_proj_w": nrm((L, PLE_DIM, D_MODEL), PLE_DIM ** -0.5),
        "final_norm": gain((D_MODEL,)),
    }


def reference(x, p, mix_norm, w_in, conv_w, conv_b, conv_ln_g, conv_ln_b,
              rwkv_mu, rwkv_w0, rwkv_w2, rwkv_a0, rwkv_a2, rwkv_g2, rwkv_kk, rwkv_ka,
              rwkv_rk, rwkv_gn_g, rwkv_gn_b, rwkv_v0, rwkv_v1, rwkv_v2, w_out,
              ffn_norm, router_group_w, router_group_b, router_expert_w, router_expert_b,
              expert_w_gate, expert_w_up, expert_w_down, ple_norm, ple_gate_w, ple_proj_w,
              final_norm):
    h = x
    v_first = None
    for i in range(DEPTH):
        u = rms_norm(h, mix_norm[i])
        z = u @ w_in[i]
        z_conv = z[..., :2 * CONV_CH]
        z_rwkv = z[..., 2 * CONV_CH:]
        y_conv = conformer_conv(z_conv, conv_w[i], conv_b[i], conv_ln_g[i], conv_ln_b[i])
        vres = None if i == 0 else (rwkv_v0[i - 1], rwkv_v1[i - 1], rwkv_v2[i - 1])
        y_rwkv, v_first = rwkv7_time_mix(
            z_rwkv, v_first, rwkv_mu[i], rwkv_w0[i], rwkv_w2[i], rwkv_a0[i], rwkv_a2[i],
            rwkv_g2[i], rwkv_kk[i], rwkv_ka[i], rwkv_rk[i], rwkv_gn_g[i], rwkv_gn_b[i], vres)
        h = h + jnp.concatenate([y_conv, y_rwkv], axis=-1) @ w_out[i]
        h = h + hierarchical_moe(rms_norm(h, ffn_norm[i]), router_group_w[i], router_group_b[i],
                                 router_expert_w[i], router_expert_b[i], expert_w_gate[i],
                                 expert_w_up[i], expert_w_down[i])
        gate = jax.nn.sigmoid(rms_norm(h, ple_norm[i]) @ ple_gate_w[i])
        h = h + gate * (p[i] @ ple_proj_w[i])
    return rms_norm(h, final_norm)
```

```python
import functools

import jax
import jax.numpy as jnp
from jax import lax
from jax.experimental import pallas as pl
from jax.experimental.pallas import tpu as pltpu

F32 = jnp.float32
BF16 = jnp.bfloat16

D_MODEL = 1024
CONV_CH = 512
CONV_WIDTH = 31
CONV_HALO = 32
RWKV_CH = 512
RWKV_HEAD = 64
LORA_W = 32
LORA_A = 32
LORA_G = 96
LORA_PAD = 256
RWKV_COLS = 3 * RWKV_CH + LORA_PAD
LORA_V_PAD = 128
N_GROUPS = 4
EXPERTS_PER_GROUP = 8
N_EXPERTS = N_GROUPS * EXPERTS_PER_GROUP
ROUTER_COLS = 128
EXPERT_FF = 512
MOE_BLOCK = 512
PLE_DIM = 256
NORM_EPS = 1e-6
LN_EPS = 1e-5
GN_EPS = 64e-5

CHUNK = 64
GROUP_HEADS = 4
GROUP_W = GROUP_HEADS * RWKV_HEAD
INV_BLOCK = 16

VMEM_LIMIT = 56 * 1024 * 1024


def _sigmoid(x):
    return 1.0 / (1.0 + jnp.exp(-x))


def _rms(x, g):
    ms = jnp.mean(x * x, axis=-1, keepdims=True)
    return x * lax.rsqrt(ms + NORM_EPS) * g


def _dot(a, b):
    return jnp.dot(a.astype(BF16), b.astype(BF16), preferred_element_type=F32)


def _dot_nt(a, b):
    return lax.dot_general(a.astype(BF16), b.astype(BF16), (((1,), (1,)), ((), ())),
                           preferred_element_type=F32)


def _dot_tn(a, b):
    return lax.dot_general(a.astype(BF16), b.astype(BF16), (((0,), (0,)), ((), ())),
                           preferred_element_type=F32)


def _split(x):
    hi = x.astype(BF16)
    lo = (x - hi.astype(F32)).astype(BF16)
    return hi, lo


def _dot_exact_rhs(a, b_bf16):
    hi, lo = _split(a)
    return (jnp.dot(hi, b_bf16, preferred_element_type=F32)
            + jnp.dot(lo, b_bf16, preferred_element_type=F32))


def _iota2(shape, dim):
    return lax.broadcasted_iota(jnp.int32, shape, dim)


CONV_ROWS = 64


def _conv_kernel(h_ref, g_ref, w_ref, cw_ref, cb_ref, lg_ref, lb_ref, o_ref, ubuf):
    ts = h_ref.shape[0]
    s = pl.program_id(1)

    @pl.when(s == 0)
    def _():
        ubuf[0:CONV_HALO, :] = jnp.zeros((CONV_HALO, CONV_CH), F32)

    @pl.when(s > 0)
    def _():
        ubuf[0:CONV_HALO, :] = ubuf[ts:ts + CONV_HALO, :]

    u = _rms(h_ref[...], g_ref[...]).astype(BF16)
    z = jnp.dot(u, w_ref[...], preferred_element_type=F32)
    ubuf[CONV_HALO:CONV_HALO + ts, :] = z[:, :CONV_CH] * _sigmoid(z[:, CONV_CH:])

    base = CONV_HALO - (CONV_WIDTH - 1)
    for c in range(ts // CONV_ROWS):
        r0 = c * CONV_ROWS
        acc = jnp.broadcast_to(cb_ref[...], (CONV_ROWS, CONV_CH))
        for j in range(CONV_WIDTH):
            acc = acc + cw_ref[j:j + 1, :] * ubuf[r0 + base + j:r0 + base + j + CONV_ROWS, :]
        mean = jnp.mean(acc, axis=-1, keepdims=True)
        cen = acc - mean
        var = jnp.mean(cen * cen, axis=-1, keepdims=True)
        un = cen * lax.rsqrt(var + LN_EPS) * lg_ref[...] + lb_ref[...]
        o_ref[r0:r0 + CONV_ROWS, :] = (un * _sigmoid(un)).astype(o_ref.dtype)


def _conv_branch(h, g, w_conv, cw, cb, lg, lb):
    b, s, d = h.shape
    ts = min(512, s)
    full = lambda shape: pl.BlockSpec(shape, lambda i, j: (0,) * len(shape))
    return pl.pallas_call(
        _conv_kernel,
        grid=(b, s // ts),
        in_specs=[
            pl.BlockSpec((None, ts, d), lambda i, j: (i, j, 0)),
            full((1, d)),
            full((d, 2 * CONV_CH)),
            full((CONV_HALO, CONV_CH)),
            full((1, CONV_CH)),
            full((1, CONV_CH)),
            full((1, CONV_CH)),
        ],
        out_specs=pl.BlockSpec((None, ts, CONV_CH), lambda i, j: (i, j, 0)),
        out_shape=jax.ShapeDtypeStruct((b, s, CONV_CH), BF16),
        scratch_shapes=[pltpu.VMEM((ts + CONV_HALO, CONV_CH), F32)],
        compiler_params=pltpu.CompilerParams(
            dimension_semantics=("arbitrary", "arbitrary"), vmem_limit_bytes=VMEM_LIMIT),
        name="conv_branch",
    )(h, g, w_conv, cw, cb, lg, lb)


def _bd_rows(y, n, w):
    r = y.shape[0]
    yt = jnp.concatenate([y] * n, axis=0)
    keep = (_iota2(yt.shape, 0) // r) == (_iota2(yt.shape, 1) // w)
    return jnp.where(keep, yt, jnp.zeros_like(yt))


def _pp(x, y):
    n = x.shape[1] // y.shape[0]
    return jnp.dot(x.astype(BF16), _bd_rows(y.astype(BF16), n, y.shape[1] // n),
                   preferred_element_type=F32)


def _tri_inv(a, eye, blk):
    a_d = jnp.where(blk, a, 0.0)
    a_o = a - a_d
    x = a_d
    d = eye + x
    for _ in range(3):
        x = _pp(x, x)
        d = d + _pp(d, x)
    n1 = _pp(d, a_o)
    n2 = _pp(n1, n1)
    t = eye + n1 + n2 + _pp(n1, n2)
    return _pp(t, d)


def _rwkv_kernel(has_vres, *refs):
    it = iter(refs)
    h_ref, g_ref, w_ref, mu_ref = next(it), next(it), next(it), next(it)
    w0_ref, a0_ref, wl_ref = next(it), next(it), next(it)
    kk_ref, ka_ref, rk_ref, gng_ref, gnb_ref = next(it), next(it), next(it), next(it), next(it)
    if has_vres:
        vf_ref, v0_ref, v1_ref, v2_ref = next(it), next(it), next(it), next(it)
    y_ref = next(it)
    vfo_ref = None if has_vres else next(it)
    zbuf, st_ref, r_s, k_s, v_s, lw_s, a_s, b_s, y_s = (next(it) for _ in range(9))

    ts = h_ref.shape[0]
    s = pl.program_id(1)
    ch = RWKV_CH
    n_grp = ch // GROUP_W

    @pl.when(s == 0)
    def _():
        zbuf[0:8, :] = jnp.zeros((8, RWKV_COLS), F32)
        st_ref[...] = jnp.zeros(st_ref.shape, F32)

    @pl.when(s > 0)
    def _():
        zbuf[0:8, :] = zbuf[ts:ts + 8, :]

    u = _rms(h_ref[...], g_ref[...]).astype(BF16)
    z = jnp.dot(u, w_ref[...], preferred_element_type=F32)
    zbuf[8:8 + ts, :] = z
    zprev = zbuf[7:7 + ts, :]
    zs = z + (zprev - z) * mu_ref[...]

    r = zs[:, 0:ch]
    k = zs[:, ch:2 * ch]
    v = zs[:, 2 * ch:3 * ch]
    lo = zs[:, 3 * ch:3 * ch + LORA_PAD]
    li = _iota2(lo.shape, 1)
    act = jnp.where(li < LORA_W, jnp.tanh(lo),
                    jnp.where(li < LORA_W + LORA_A, lo, _sigmoid(lo)))
    lora = _dot(act, wl_ref[...])
    w_pre = w0_ref[...] + lora[:, 0:ch]
    a = _sigmoid(a0_ref[...] + lora[:, ch:2 * ch])
    g = lora[:, 2 * ch:3 * ch]
    nw = -w_pre
    softplus = jnp.maximum(nw, 0.0) + jnp.log(1.0 + jnp.exp(-jnp.abs(nw)))
    logw = -jnp.exp(-softplus - 0.5)

    if has_vres:
        vv = _dot(_dot(v, v1_ref[...]), v2_ref[...])
        v = v + (vf_ref[...] - v) * _sigmoid(v0_ref[...] + vv)
    else:
        vfo_ref[...] = v

    ones_bd = ((_iota2((ch, ch), 0) // RWKV_HEAD) == (_iota2((ch, ch), 1) // RWKV_HEAD)).astype(BF16)
    kk = k * kk_ref[...]
    ss = _dot(kk * kk, ones_bd)
    kk = kk / jnp.maximum(jnp.sqrt(ss), 1e-12)
    k = k * (1.0 + (a - 1.0) * ka_ref[...])

    r_s[...] = r
    k_s[...] = k
    v_s[...] = v
    lw_s[...] = logw
    a_s[...] = -kk
    b_s[...] = kk * a

    c = CHUNK
    gw = GROUP_W
    row = _iota2((c, gw), 0)
    col = _iota2((c, gw), 1) % c
    strict = col < row
    incl = col <= row
    eye = (col == row).astype(F32)
    blk = (col // INV_BLOCK) == (row // INV_BLOCK)
    tri = (_iota2((c, c), 1) <= _iota2((c, c), 0)).astype(BF16)
    st_keep = (_iota2((gw, gw), 0) // RWKV_HEAD) == (_iota2((gw, gw), 1) // RWKV_HEAD)

    def chunk(ci, carry):
        off = pl.multiple_of(ci * c, c)
        lw = lw_s[pl.ds(off, c), :]
        cum = _dot_exact_rhs_lhs(tri, lw)
        tot = cum[c - 1:c, :]
        p_in = jnp.exp(cum)
        p_prev = jnp.exp(cum - lw)
        p_inv = jnp.exp(-cum)
        p_end = jnp.exp(tot - cum)
        p_all = jnp.exp(tot)
        rr = r_s[pl.ds(off, c), :]
        kc = k_s[pl.ds(off, c), :]
        vc = v_s[pl.ds(off, c), :].astype(BF16)
        ac = a_s[pl.ds(off, c), :]
        bc = b_s[pl.ds(off, c), :]
        at = (ac * p_prev).astype(BF16)
        rt = (rr * p_in).astype(BF16)
        bt = (bc * p_inv).astype(BF16)
        kt = (kc * p_inv).astype(BF16)
        bh = (bc * p_end).astype(BF16)
        kh = (kc * p_end).astype(BF16)
        ys = []
        for gi in range(n_grp):
            sl = slice(gi * gw, (gi + 1) * gw)
            ar = jnp.concatenate([at[:, sl], rt[:, sl]], axis=0)
            m_b = _dot_nt(ar, _bd_rows(bt[:, sl], GROUP_HEADS, RWKV_HEAD))
            m_k = _dot_nt(ar, _bd_rows(kt[:, sl], GROUP_HEADS, RWKV_HEAD))
            a_ab = jnp.where(strict, m_b[:c], 0.0)
            a_ak = jnp.where(strict, m_k[:c], 0.0)
            a_rb = jnp.where(incl, m_b[c:], 0.0)
            a_rk = jnp.where(incl, m_k[c:], 0.0)
            t_inv = _tri_inv(a_ab, eye, blk)
            st = st_ref[gi]
            xs = _dot_nt(ar, st)
            vg = vc[:, sl]
            rhs = xs[:c] + _pp(a_ak, vg)
            uu = _pp(t_inv, rhs)
            ys.append(xs[c:] + _pp(a_rb, uu) + _pp(a_rk, vg))
            upd = _dot_tn(jnp.concatenate([uu.astype(BF16), vg], axis=0),
                          jnp.concatenate([bh[:, sl], kh[:, sl]], axis=0))
            st_ref[gi] = st * p_all[:, sl] + jnp.where(st_keep, upd, 0.0)
        y_s[pl.ds(off, c), :] = jnp.concatenate(ys, axis=1)
        return carry

    lax.fori_loop(0, ts // c, chunk, 0)

    y = y_s[...]
    mean = _dot(y, ones_bd) * (1.0 / RWKV_HEAD)
    cen = y - mean
    var = _dot(cen * cen, ones_bd) * (1.0 / RWKV_HEAD)
    yn = cen * lax.rsqrt(var + GN_EPS) * gng_ref[...] + gnb_ref[...]
    bonus = _dot(r * k * rk_ref[...], ones_bd)
    y_ref[...] = ((yn + bonus * v) * g).astype(y_ref.dtype)


def _dot_exact_rhs_lhs(tri_bf16, x):
    hi, lo = _split(x)
    return (jnp.dot(tri_bf16, hi, preferred_element_type=F32)
            + jnp.dot(tri_bf16, lo, preferred_element_type=F32))


def _rwkv_branch(h, g, w_rwkv, mu, w0, a0, wl, kk, ka, rk, gng, gnb, vres):
    b, s, d = h.shape
    ts = min(256, s)
    has_vres = vres is not None
    full = lambda shape: pl.BlockSpec(shape, lambda i, j: (0,) * len(shape))
    row_blk = lambda width: pl.BlockSpec((None, ts, width), lambda i, j: (i, j, 0))
    vec = full((1, RWKV_CH))
    in_specs = [row_blk(d), full((1, d)), full((d, RWKV_COLS)), full((1, RWKV_COLS)),
                vec, vec, full((LORA_PAD, 3 * RWKV_CH)), vec, vec, vec, vec, vec]
    args = [h, g, w_rwkv, mu, w0, a0, wl, kk, ka, rk, gng, gnb]
    if has_vres:
        vf, v0, v1, v2 = vres
        in_specs += [row_blk(RWKV_CH), vec, full((RWKV_CH, LORA_V_PAD)), full((LORA_V_PAD, RWKV_CH))]
        args += [vf, v0, v1, v2]
        out_specs = row_blk(RWKV_CH)
        out_shape = jax.ShapeDtypeStruct((b, s, RWKV_CH), BF16)
    else:
        out_specs = [row_blk(RWKV_CH), row_blk(RWKV_CH)]
        out_shape = [jax.ShapeDtypeStruct((b, s, RWKV_CH), BF16),
                     jax.ShapeDtypeStruct((b, s, RWKV_CH), F32)]
    big = pltpu.VMEM((ts, RWKV_CH), F32)
    return pl.pallas_call(
        functools.partial(_rwkv_kernel, has_vres),
        grid=(b, s // ts),
        in_specs=in_specs,
        out_specs=out_specs,
        out_shape=out_shape,
        scratch_shapes=[pltpu.VMEM((ts + 8, RWKV_COLS), F32),
                        pltpu.VMEM((RWKV_CH // GROUP_W, GROUP_W, GROUP_W), F32),
                        big, big, big, big, big, big, big],
        compiler_params=pltpu.CompilerParams(
            dimension_semantics=("arbitrary", "arbitrary"), vmem_limit_bytes=VMEM_LIMIT),
        name="rwkv_branch",
    )(*args)


def _out_router_kernel(h_ref, yc_ref, yr_ref, wo_ref, g_ref, rwh_ref, rwl_ref, rb_ref,
                       h1_ref, u_ref, e_ref, gate_ref):
    wo = wo_ref[...]
    h1 = (h_ref[...] + jnp.dot(yc_ref[...], wo[:CONV_CH], preferred_element_type=F32)
          + jnp.dot(yr_ref[...], wo[CONV_CH:], preferred_element_type=F32))
    h1_ref[...] = h1
    u = _rms(h1, g_ref[...])
    u_ref[...] = u
    u_hi, u_lo = _split(u)
    rwh = rwh_ref[...]
    logits = (jnp.dot(u_hi, rwh, preferred_element_type=F32)
              + jnp.dot(u_lo, rwh, preferred_element_type=F32)
              + jnp.dot(u_hi, rwl_ref[...], preferred_element_type=F32)) + rb_ref[...]
    li = _iota2(logits.shape, 1)
    neg = jnp.float32(-jnp.inf)
    big = jnp.int32(ROUTER_COLS)
    gl = jnp.where(li < N_GROUPS, logits, neg)
    gmax = jnp.max(gl, axis=-1, keepdims=True)
    g_idx = jnp.min(jnp.where(gl == gmax, li, big), axis=-1, keepdims=True)
    g_p = 1.0 / jnp.sum(jnp.exp(gl - gmax), axis=-1, keepdims=True)
    e_lo = N_GROUPS + g_idx * EXPERTS_PER_GROUP
    el = jnp.where((li >= e_lo) & (li < e_lo + EXPERTS_PER_GROUP), logits, neg)
    v1 = jnp.max(el, axis=-1, keepdims=True)
    i1 = jnp.min(jnp.where(el == v1, li, big), axis=-1, keepdims=True)
    el2 = jnp.where(li == i1, neg, el)
    v2 = jnp.max(el2, axis=-1, keepdims=True)
    i2 = jnp.min(jnp.where(el2 == v2, li, big), axis=-1, keepdims=True)
    ex = jnp.exp(v2 - v1)
    w1 = g_p / (1.0 + ex)
    w2 = g_p * ex / (1.0 + ex)
    e_ref[...] = jnp.concatenate([i1, i2], axis=1) - N_GROUPS
    gate_ref[...] = jnp.concatenate([w1, w2], axis=1)


def _out_router(h, yc, yr, wo, g, rwh, rwl, rb):
    t, d = h.shape
    tm = min(512, t)
    full = lambda shape: pl.BlockSpec(shape, lambda i: (0,) * len(shape))
    rows = lambda width: pl.BlockSpec((tm, width), lambda i: (i, 0))
    return pl.pallas_call(
        _out_router_kernel,
        grid=(t // tm,),
        in_specs=[rows(d), rows(CONV_CH), rows(RWKV_CH), full((d, d)), full((1, d)),
                  full((d, ROUTER_COLS)), full((d, ROUTER_COLS)), full((1, ROUTER_COLS))],
        out_specs=[rows(d), rows(d), rows(2), rows(2)],
        out_shape=[jax.ShapeDtypeStruct((t, d), F32), jax.ShapeDtypeStruct((t, d), F32),
                   jax.ShapeDtypeStruct((t, 2), jnp.int32), jax.ShapeDtypeStruct((t, 2), F32)],
        compiler_params=pltpu.CompilerParams(
            dimension_semantics=("arbitrary",), vmem_limit_bytes=VMEM_LIMIT),
        name="out_router",
    )(h, yc, yr, wo, g, rwh, rwl, rb)


def _row_copy(src_hbm, src_row, dst, dst_row, sem):
    return pltpu.make_async_copy(src_hbm.at[pl.ds(src_row, 1), :], dst.at[pl.ds(dst_row, 1), :], sem)


def _expert_kernel(be_ref, cnt_ref, tok_ref, gate_ref, u_hbm, wg_ref, wu_ref, wd_ref, o_ref, xbuf, sem):
    i = pl.program_id(0)
    nrows = xbuf.shape[0]

    @pl.when(cnt_ref[i] > 0)
    def _():
        def issue(r, carry):
            _row_copy(u_hbm, tok_ref[0, r], xbuf, r, sem).start()
            return carry

        lax.fori_loop(0, nrows, issue, 0)

        def wait(r, carry):
            _row_copy(u_hbm, 0, xbuf, r, sem).wait()
            return carry

        lax.fori_loop(0, nrows, wait, 0)
        x = xbuf[...].astype(BF16)
        hg = jnp.dot(x, wg_ref[...], preferred_element_type=F32)
        hu = jnp.dot(x, wu_ref[...], preferred_element_type=F32)
        hid = (hg * _sigmoid(hg) * hu).astype(BF16)
        o_ref[...] = jnp.dot(hid, wd_ref[...], preferred_element_type=F32) * gate_ref[...]

    @pl.when(cnt_ref[i] <= 0)
    def _():
        o_ref[...] = jnp.zeros(o_ref.shape, o_ref.dtype)


def _expert_mlp(blk_e, blk_cnt, buf_tok, buf_gate, u, wg, wu, wd):
    n_blocks = blk_e.shape[0]
    d = u.shape[1]
    grid_spec = pltpu.PrefetchScalarGridSpec(
        num_scalar_prefetch=2,
        grid=(n_blocks,),
        in_specs=[
            pl.BlockSpec((None, 1, MOE_BLOCK), lambda i, be, bc: (i, 0, 0), memory_space=pltpu.SMEM),
            pl.BlockSpec((MOE_BLOCK, 1), lambda i, be, bc: (i, 0)),
            pl.BlockSpec(memory_space=pl.ANY),
            pl.BlockSpec((None, d, EXPERT_FF), lambda i, be, bc: (be[i], 0, 0)),
            pl.BlockSpec((None, d, EXPERT_FF), lambda i, be, bc: (be[i], 0, 0)),
            pl.BlockSpec((None, EXPERT_FF, d), lambda i, be, bc: (be[i], 0, 0)),
        ],
        out_specs=pl.BlockSpec((MOE_BLOCK, d), lambda i, be, bc: (i, 0)),
        scratch_shapes=[pltpu.VMEM((MOE_BLOCK, d), F32), pltpu.SemaphoreType.DMA],
    )
    return pl.pallas_call(
        _expert_kernel,
        grid_spec=grid_spec,
        out_shape=jax.ShapeDtypeStruct((n_blocks * MOE_BLOCK, d), F32),
        compiler_params=pltpu.CompilerParams(
            dimension_semantics=("arbitrary",), vmem_limit_bytes=VMEM_LIMIT),
        name="expert_mlp",
    )(blk_e, blk_cnt, buf_tok.reshape(n_blocks, 1, MOE_BLOCK), buf_gate.reshape(-1, 1), u, wg, wu, wd)


def _combine_kernel(final, pos_ref, h_ref, p_ref, y_hbm, g_ref, wg_ref, wp_ref, fg_ref, o_ref,
                    y0, y1, sem):
    tm = h_ref.shape[0]

    def issue(r, carry):
        _row_copy(y_hbm, pos_ref[0, r], y0, r, sem).start()
        _row_copy(y_hbm, pos_ref[0, tm + r], y1, r, sem).start()
        return carry

    lax.fori_loop(0, tm, issue, 0)

    def wait(r, carry):
        _row_copy(y_hbm, 0, y0, r, sem).wait()
        _row_copy(y_hbm, 0, y1, r, sem).wait()
        return carry

    lax.fori_loop(0, tm, wait, 0)
    h2 = h_ref[...] + y0[...] + y1[...]
    u = _rms(h2, g_ref[...]).astype(BF16)
    gate = _sigmoid(jnp.dot(u, wg_ref[...], preferred_element_type=F32))
    pp = jnp.dot(p_ref[...].astype(BF16), wp_ref[...], preferred_element_type=F32)
    h3 = h2 + gate * pp
    if final:
        h3 = _rms(h3, fg_ref[...])
    o_ref[...] = h3


def _combine_ple(final, pos, h1, p, ybuf, g, wg, wp, fg):
    t, d = h1.shape
    tm = min(256, t)
    nb = t // tm
    pos_blk = jnp.concatenate([pos[:, 0].reshape(nb, 1, tm), pos[:, 1].reshape(nb, 1, tm)], axis=2)
    full = lambda shape: pl.BlockSpec(shape, lambda i: (0,) * len(shape))
    rows = lambda width: pl.BlockSpec((tm, width), lambda i: (i, 0))
    return pl.pallas_call(
        functools.partial(_combine_kernel, final),
        grid=(nb,),
        in_specs=[
            pl.BlockSpec((None, 1, 2 * tm), lambda i: (i, 0, 0), memory_space=pltpu.SMEM),
            rows(d), rows(PLE_DIM),
            pl.BlockSpec(memory_space=pl.ANY),
            full((1, d)), full((d, d)), full((PLE_DIM, d)), full((1, d)),
        ],
        out_specs=rows(d),
        out_shape=jax.ShapeDtypeStruct((t, d), F32),
        scratch_shapes=[pltpu.VMEM((tm, d), F32), pltpu.VMEM((tm, d), F32), pltpu.SemaphoreType.DMA],
        compiler_params=pltpu.CompilerParams(
            dimension_semantics=("arbitrary",), vmem_limit_bytes=VMEM_LIMIT),
        name="combine_ple",
    )(pos_blk, h1, p, ybuf, g, wg, wp, fg)


def _route_plan(e_idx, gate_w):
    t = e_idx.shape[0]
    a = t * 2
    n_blocks = -(-a // MOE_BLOCK) + N_EXPERTS
    p = n_blocks * MOE_BLOCK
    flat_e = e_idx.reshape(-1)
    order = jnp.argsort(flat_e)
    se = flat_e[order]
    counts = jnp.zeros((N_EXPERTS,), jnp.int32).at[flat_e].add(1)
    starts = jnp.cumsum(counts) - counts
    pcounts = (counts + MOE_BLOCK - 1) // MOE_BLOCK * MOE_BLOCK
    pend = jnp.cumsum(pcounts)
    pstarts = pend - pcounts
    dest = (pstarts[se] + (jnp.arange(a, dtype=jnp.int32) - starts[se])).astype(jnp.int32)
    buf_tok = jnp.zeros((p,), jnp.int32).at[dest].set((order // 2).astype(jnp.int32))
    buf_gate = jnp.zeros((p,), F32).at[dest].set(gate_w.reshape(-1)[order])
    pos = jnp.zeros((a,), jnp.int32).at[order].set(dest).reshape(t, 2)
    blk_start = jnp.arange(n_blocks, dtype=jnp.int32) * MOE_BLOCK
    blk_e = jnp.minimum(jnp.searchsorted(pend, blk_start, side='right'), N_EXPERTS - 1).astype(jnp.int32)
    blk_cnt = jnp.clip(counts[blk_e] - (blk_start - pstarts[blk_e]), 0, MOE_BLOCK).astype(jnp.int32)
    return blk_e, blk_cnt, buf_tok, buf_gate, pos


def kernel(x, p, mix_norm, w_in, conv_w, conv_b, conv_ln_g, conv_ln_b, rwkv_mu, rwkv_w0, rwkv_w2, rwkv_a0, rwkv_a2, rwkv_g2, rwkv_kk, rwkv_ka, rwkv_rk, rwkv_gn_g, rwkv_gn_b, rwkv_v0, rwkv_v1, rwkv_v2, w_out, ffn_norm, router_group_w, router_group_b, router_expert_w, router_expert_b, expert_w_gate, expert_w_up, expert_w_down, ple_norm, ple_gate_w, ple_proj_w, final_norm):
    b, s, d = x.shape
    depth = w_in.shape[0]
    t = b * s
    n_lora = LORA_W + LORA_A + LORA_G
    row = lambda v: v.reshape(1, -1).astype(F32)

    h = x
    v_first = None
    for i in range(depth):
        w_conv = w_in[i, :, :2 * CONV_CH].astype(BF16)
        w_rwkv = jnp.pad(w_in[i, :, 2 * CONV_CH:], ((0, 0), (0, LORA_PAD - n_lora))).astype(BF16)
        mu = jnp.pad(rwkv_mu[i], (0, LORA_PAD - n_lora)).reshape(1, -1)
        cw = jnp.pad(conv_w[i], ((0, CONV_HALO - CONV_WIDTH), (0, 0)))
        yc = _conv_branch(h, row(mix_norm[i]), w_conv, cw, row(conv_b[i]), row(conv_ln_g[i]),
                          row(conv_ln_b[i]))

        wl = jnp.zeros((LORA_PAD, 3 * RWKV_CH), F32)
        wl = wl.at[0:LORA_W, 0:RWKV_CH].set(rwkv_w2[i])
        wl = wl.at[LORA_W:LORA_W + LORA_A, RWKV_CH:2 * RWKV_CH].set(rwkv_a2[i])
        wl = wl.at[LORA_W + LORA_A:n_lora, 2 * RWKV_CH:].set(rwkv_g2[i])
        if i == 0:
            vres = None
        else:
            v1 = jnp.pad(rwkv_v1[i - 1], ((0, 0), (0, LORA_V_PAD - rwkv_v1.shape[2]))).astype(BF16)
            v2 = jnp.pad(rwkv_v2[i - 1], ((0, LORA_V_PAD - rwkv_v2.shape[1]), (0, 0))).astype(BF16)
            vres = (v_first, row(rwkv_v0[i - 1]), v1, v2)
        res = _rwkv_branch(h, row(mix_norm[i]), w_rwkv, mu, row(rwkv_w0[i]), row(rwkv_a0[i]),
                           wl.astype(BF16), row(rwkv_kk[i]), row(rwkv_ka[i]), row(rwkv_rk[i]),
                           row(rwkv_gn_g[i]), row(rwkv_gn_b[i]), vres)
        if i == 0:
            yr, v_first = res
        else:
            yr = res

        rw = jnp.concatenate(
            [router_group_w[i]] + [router_expert_w[i, gidx] for gidx in range(N_GROUPS)], axis=1)
        rw = jnp.pad(rw, ((0, 0), (0, ROUTER_COLS - rw.shape[1])))
        rw_hi = rw.astype(BF16)
        rw_lo = (rw - rw_hi.astype(F32)).astype(BF16)
        rb = jnp.concatenate([router_group_b[i], router_expert_b[i].reshape(-1)])
        rb = jnp.pad(rb, (0, ROUTER_COLS - rb.shape[0])).reshape(1, -1)
        h1, u, e_idx, gate_w = _out_router(
            h.reshape(t, d), yc.reshape(t, CONV_CH), yr.reshape(t, RWKV_CH),
            w_out[i].astype(BF16), row(ffn_norm[i]), rw_hi, rw_lo, rb)

        blk_e, blk_cnt, buf_tok, buf_gate, pos = _route_plan(e_idx, gate_w)
        ybuf = _expert_mlp(blk_e, blk_cnt, buf_tok, buf_gate, u,
                           expert_w_gate[i].astype(BF16), expert_w_up[i].astype(BF16),
                           expert_w_down[i].astype(BF16))
        h = _combine_ple(i == depth - 1, pos, h1, p[i].reshape(t, PLE_DIM), ybuf,
                         row(ple_norm[i]), ple_gate_w[i].astype(BF16), ple_proj_w[i].astype(BF16),
                         row(final_norm)).reshape(b, s, d)
    return h
```

```python
import functools

import jax
import jax.numpy as jnp
from jax import lax
from jax.experimental import pallas as pl
from jax.experimental.pallas import tpu as pltpu

F32 = jnp.float32
BF16 = jnp.bfloat16

D_MODEL = 1024
CONV_CH = 512
CONV_WIDTH = 31
CONV_HALO = 32
RWKV_CH = 512
RWKV_HEAD = 64
LORA_W = 32
LORA_A = 32
LORA_G = 96
LORA_PAD = 256
RWKV_COLS = 3 * RWKV_CH + LORA_PAD
LORA_V_PAD = 128
N_GROUPS = 4
EXPERTS_PER_GROUP = 8
N_EXPERTS = N_GROUPS * EXPERTS_PER_GROUP
ROUTER_COLS = 128
EXPERT_FF = 512
MOE_BLOCK = 512
LANES = 128
ROW_TILE = 8
PLE_DIM = 256
NORM_EPS = 1e-6
LN_EPS = 1e-5
GN_EPS = 64e-5

CHUNK = 64
GROUP_HEADS = 4
GROUP_W = GROUP_HEADS * RWKV_HEAD
INV_BLOCK = 16

VMEM_LIMIT = 56 * 1024 * 1024


def _sigmoid(x):
    return 1.0 / (1.0 + jnp.exp(-x))


def _rms(x, g):
    ms = jnp.mean(x * x, axis=-1, keepdims=True)
    return x * lax.rsqrt(ms + NORM_EPS) * g


def _dot(a, b):
    return jnp.dot(a.astype(BF16), b.astype(BF16), preferred_element_type=F32)


def _dot_nt(a, b):
    return lax.dot_general(a.astype(BF16), b.astype(BF16), (((1,), (1,)), ((), ())),
                           preferred_element_type=F32)


def _dot_tn(a, b):
    return lax.dot_general(a.astype(BF16), b.astype(BF16), (((0,), (0,)), ((), ())),
                           preferred_element_type=F32)


def _split(x):
    hi = x.astype(BF16)
    lo = (x - hi.astype(F32)).astype(BF16)
    return hi, lo


def _iota2(shape, dim):
    return lax.broadcasted_iota(jnp.int32, shape, dim)


CONV_ROWS = 64


def _conv_kernel(h_ref, g_ref, w_ref, cw_ref, cb_ref, lg_ref, lb_ref, o_ref, ubuf, shifted):
    ts = h_ref.shape[0]
    s = pl.program_id(1)

    @pl.when(s == 0)
    def _():
        ubuf[0:CONV_HALO, :] = jnp.zeros((CONV_HALO, CONV_CH), F32)

    @pl.when(s > 0)
    def _():
        ubuf[0:CONV_HALO, :] = ubuf[ts:ts + CONV_HALO, :]

    u = _rms(h_ref[...], g_ref[...]).astype(BF16)
    z = jnp.dot(u, w_ref[...], preferred_element_type=F32)
    ubuf[CONV_HALO:CONV_HALO + ts, :] = z[:, :CONV_CH] * _sigmoid(z[:, CONV_CH:])

    span = ts + CONV_HALO - ROW_TILE
    for ph in range(1, ROW_TILE):
        shifted[ph - 1, 0:span, :] = ubuf[ph:ph + span, :]

    base = CONV_HALO - (CONV_WIDTH - 1)
    for c in range(ts // CONV_ROWS):
        r0 = c * CONV_ROWS
        acc = jnp.broadcast_to(cb_ref[...], (CONV_ROWS, CONV_CH))
        for j in range(CONV_WIDTH):
            ph, lo = (base + j) % ROW_TILE, r0 + (base + j) // ROW_TILE * ROW_TILE
            src = ubuf[lo:lo + CONV_ROWS, :] if ph == 0 else shifted[ph - 1, lo:lo + CONV_ROWS, :]
            acc = acc + cw_ref[j:j + 1, :] * src
        mean = jnp.mean(acc, axis=-1, keepdims=True)
        cen = acc - mean
        var = jnp.mean(cen * cen, axis=-1, keepdims=True)
        un = cen * lax.rsqrt(var + LN_EPS) * lg_ref[...] + lb_ref[...]
        o_ref[r0:r0 + CONV_ROWS, :] = (un * _sigmoid(un)).astype(o_ref.dtype)


def _conv_branch(h, g, w_conv, cw, cb, lg, lb):
    b, s, d = h.shape
    ts = min(512, s)
    full = lambda shape: pl.BlockSpec(shape, lambda i, j: (0,) * len(shape))
    return pl.pallas_call(
        _conv_kernel,
        grid=(b, s // ts),
        in_specs=[
            pl.BlockSpec((None, ts, d), lambda i, j: (i, j, 0)),
            full((1, d)),
            full((d, 2 * CONV_CH)),
            full((CONV_HALO, CONV_CH)),
            full((1, CONV_CH)),
            full((1, CONV_CH)),
            full((1, CONV_CH)),
        ],
        out_specs=pl.BlockSpec((None, ts, CONV_CH), lambda i, j: (i, j, 0)),
        out_shape=jax.ShapeDtypeStruct((b, s, CONV_CH), BF16),
        scratch_shapes=[pltpu.VMEM((ts + CONV_HALO, CONV_CH), F32),
                        pltpu.VMEM((ROW_TILE - 1, ts + CONV_HALO, CONV_CH), F32)],
        compiler_params=pltpu.CompilerParams(
            dimension_semantics=("arbitrary", "arbitrary"), vmem_limit_bytes=VMEM_LIMIT),
        name="conv_branch",
    )(h, g, w_conv, cw, cb, lg, lb)


def _bd_rows(y, n, w):
    r = y.shape[0]
    yt = jnp.concatenate([y] * n, axis=0)
    keep = (_iota2(yt.shape, 0) // r) == (_iota2(yt.shape, 1) // w)
    return jnp.where(keep, yt, jnp.zeros_like(yt))


def _pp(x, y):
    n = x.shape[1] // y.shape[0]
    return jnp.dot(x.astype(BF16), _bd_rows(y.astype(BF16), n, y.shape[1] // n),
                   preferred_element_type=F32)


def _tri_inv_all(a_list, eye, blk):
    a_d = [jnp.where(blk, a, 0.0) for a in a_list]
    a_o = [a - ad for a, ad in zip(a_list, a_d)]
    x = a_d
    d = [eye + xi for xi in x]
    for _ in range(3):
        x = [_pp(xi, xi) for xi in x]
        d = [di + _pp(di, xi) for di, xi in zip(d, x)]
    n1 = [_pp(di, ao) for di, ao in zip(d, a_o)]
    n2 = [_pp(n, n) for n in n1]
    e = [di + _pp(n, di) for di, n in zip(d, n1)]
    return [ei + _pp(n, ei) for ei, n in zip(e, n2)]


def _dot_exact_lhs(tri_bf16, x):
    hi, lo = _split(x)
    return (jnp.dot(tri_bf16, hi, preferred_element_type=F32)
            + jnp.dot(tri_bf16, lo, preferred_element_type=F32))


def _rwkv_kernel(has_vres, *refs):
    it = iter(refs)
    h_ref, g_ref, w_ref, mu_ref = next(it), next(it), next(it), next(it)
    w0_ref, a0_ref, wl_ref = next(it), next(it), next(it)
    kk_ref, ka_ref, rk_ref, gng_ref, gnb_ref = next(it), next(it), next(it), next(it), next(it)
    if has_vres:
        vf_ref, v0_ref, v1_ref, v2_ref = next(it), next(it), next(it), next(it)
    y_ref = next(it)
    vfo_ref = None if has_vres else next(it)
    zbuf, st_ref, r_s, k_s, v_s, lw_s, a_s, b_s, y_s = (next(it) for _ in range(9))

    ts = h_ref.shape[0]
    s = pl.program_id(1)
    ch = RWKV_CH
    n_grp = ch // GROUP_W

    @pl.when(s == 0)
    def _():
        zbuf[0:8, :] = jnp.zeros((8, RWKV_COLS), F32)
        st_ref[...] = jnp.zeros(st_ref.shape, F32)

    @pl.when(s > 0)
    def _():
        zbuf[0:8, :] = zbuf[ts:ts + 8, :]

    u = _rms(h_ref[...], g_ref[...]).astype(BF16)
    z = jnp.dot(u, w_ref[...], preferred_element_type=F32)
    zbuf[8:8 + ts, :] = z
    zprev = zbuf[7:7 + ts, :]
    zs = z + (zprev - z) * mu_ref[...]

    r = zs[:, 0:ch]
    k = zs[:, ch:2 * ch]
    v = zs[:, 2 * ch:3 * ch]
    lo = zs[:, 3 * ch:3 * ch + LORA_PAD]
    li = _iota2(lo.shape, 1)
    act = jnp.where(li < LORA_W, jnp.tanh(lo),
                    jnp.where(li < LORA_W + LORA_A, lo, _sigmoid(lo)))
    lora = _dot(act, wl_ref[...])
    w_pre = w0_ref[...] + lora[:, 0:ch]
    a = _sigmoid(a0_ref[...] + lora[:, ch:2 * ch])
    g = lora[:, 2 * ch:3 * ch]
    nw = -w_pre
    softplus = jnp.maximum(nw, 0.0) + jnp.log(1.0 + jnp.exp(-jnp.abs(nw)))
    logw = -jnp.exp(-softplus - 0.5)

    if has_vres:
        vv = _dot(_dot(v, v1_ref[...]), v2_ref[...])
        v = v + (vf_ref[...] - v) * _sigmoid(v0_ref[...] + vv)
    else:
        vfo_ref[...] = v

    ones_bd = ((_iota2((ch, ch), 0) // RWKV_HEAD) == (_iota2((ch, ch), 1) // RWKV_HEAD)).astype(BF16)
    kk = k * kk_ref[...]
    ss = _dot(kk * kk, ones_bd)
    kk = kk / jnp.maximum(jnp.sqrt(ss), 1e-12)
    k = k * (1.0 + (a - 1.0) * ka_ref[...])

    r_s[...] = r
    k_s[...] = k
    v_s[...] = v
    lw_s[...] = logw
    a_s[...] = -kk
    b_s[...] = kk * a

    c = CHUNK
    gw = GROUP_W
    row = _iota2((c, gw), 0)
    col = _iota2((c, gw), 1) % c
    strict = col < row
    incl = col <= row
    eye = (col == row).astype(F32)
    blk = (col // INV_BLOCK) == (row // INV_BLOCK)
    tri = (_iota2((c, c), 1) <= _iota2((c, c), 0)).astype(BF16)
    st_keep = (_iota2((gw, gw), 0) // RWKV_HEAD) == (_iota2((gw, gw), 1) // RWKV_HEAD)

    n_chunks = ts // c
    prob = [(ci, gi) for ci in range(n_chunks) for gi in range(n_grp)]
    at, rt, bt, kt, bh, kh, vc, p_all = [], [], [], [], [], [], [], []
    for ci in range(n_chunks):
        rows = slice(ci * c, (ci + 1) * c)
        lw = lw_s[rows, :]
        cum = _dot_exact_lhs(tri, lw)
        tot = cum[c - 1:c, :]
        p_inv = jnp.exp(-cum)
        p_end = jnp.exp(tot - cum)
        at.append((a_s[rows, :] * jnp.exp(cum - lw)).astype(BF16))
        rt.append((r_s[rows, :] * jnp.exp(cum)).astype(BF16))
        bt.append((b_s[rows, :] * p_inv).astype(BF16))
        kt.append((k_s[rows, :] * p_inv).astype(BF16))
        bh.append((b_s[rows, :] * p_end).astype(BF16))
        kh.append((k_s[rows, :] * p_end).astype(BF16))
        vc.append(v_s[rows, :].astype(BF16))
        p_all.append(jnp.exp(tot))
    grp = lambda lst: [lst[ci][:, gi * gw:(gi + 1) * gw] for ci, gi in prob]
    at, rt, bt, kt, bh, kh, vg = grp(at), grp(rt), grp(bt), grp(kt), grp(bh), grp(kh), grp(vc)
    ar = [jnp.concatenate([a, r], axis=0) for a, r in zip(at, rt)]
    m_b = [_dot_nt(x, _bd_rows(y, GROUP_HEADS, RWKV_HEAD)) for x, y in zip(ar, bt)]
    m_k = [_dot_nt(x, _bd_rows(y, GROUP_HEADS, RWKV_HEAD)) for x, y in zip(ar, kt)]
    a_ak = [jnp.where(strict, m[:c], 0.0) for m in m_k]
    a_rb = [jnp.where(incl, m[c:], 0.0) for m in m_b]
    a_rk = [jnp.where(incl, m[c:], 0.0) for m in m_k]
    t_inv = _tri_inv_all([jnp.where(strict, m[:c], 0.0) for m in m_b], eye, blk)
    akv = [_pp(a, v) for a, v in zip(a_ak, vg)]
    wt = [_pp(t, a) for t, a in zip(t_inv, at)]
    ut = [_pp(t, x) for t, x in zip(t_inv, akv)]
    r2 = [r.astype(F32) + _pp(a, w) for r, a, w in zip(rt, a_rb, wt)]
    y0 = [_pp(a, u) + _pp(b, v) for a, u, b, v in zip(a_rb, ut, a_rk, vg)]
    g_mat = [jnp.where(st_keep, _dot_tn(w, b), 0.0).astype(BF16) for w, b in zip(wt, bh)]
    h_mat = [jnp.where(st_keep, _dot_tn(jnp.concatenate([u.astype(BF16), v], axis=0),
                                        jnp.concatenate([b, k], axis=0)), 0.0)
             for u, v, b, k in zip(ut, vg, bh, kh)]
    st = [st_ref[gi] for gi in range(n_grp)]
    for ci in range(n_chunks):
        ys = []
        for gi in range(n_grp):
            q = ci * n_grp + gi
            st_b = st[gi].astype(BF16)
            ys.append(_dot_nt(r2[q], st_b) + y0[q])
            st[gi] = (st[gi] * p_all[ci][:, gi * gw:(gi + 1) * gw]
                      + jnp.dot(st_b, g_mat[q], preferred_element_type=F32) + h_mat[q])
        y_s[ci * c:(ci + 1) * c, :] = jnp.concatenate(ys, axis=1)
    for gi in range(n_grp):
        st_ref[gi] = st[gi]

    y = y_s[...]
    mean = _dot(y, ones_bd) * (1.0 / RWKV_HEAD)
    cen = y - mean
    var = _dot(cen * cen, ones_bd) * (1.0 / RWKV_HEAD)
    yn = cen * lax.rsqrt(var + GN_EPS) * gng_ref[...] + gnb_ref[...]
    bonus = _dot(r * k * rk_ref[...], ones_bd)
    y_ref[...] = ((yn + bonus * v) * g).astype(y_ref.dtype)


def _rwkv_branch(h, g, w_rwkv, mu, w0, a0, wl, kk, ka, rk, gng, gnb, vres):
    b, s, d = h.shape
    ts = min(256, s)
    has_vres = vres is not None
    full = lambda shape: pl.BlockSpec(shape, lambda i, j: (0,) * len(shape))
    row_blk = lambda width: pl.BlockSpec((None, ts, width), lambda i, j: (i, j, 0))
    vec = full((1, RWKV_CH))
    in_specs = [row_blk(d), full((1, d)), full((d, RWKV_COLS)), full((1, RWKV_COLS)),
                vec, vec, full((LORA_PAD, 3 * RWKV_CH)), vec, vec, vec, vec, vec]
    args = [h, g, w_rwkv, mu, w0, a0, wl, kk, ka, rk, gng, gnb]
    if has_vres:
        vf, v0, v1, v2 = vres
        in_specs += [row_blk(RWKV_CH), vec, full((RWKV_CH, LORA_V_PAD)), full((LORA_V_PAD, RWKV_CH))]
        args += [vf, v0, v1, v2]
        out_specs = row_blk(RWKV_CH)
        out_shape = jax.ShapeDtypeStruct((b, s, RWKV_CH), BF16)
    else:
        out_specs = [row_blk(RWKV_CH), row_blk(RWKV_CH)]
        out_shape = [jax.ShapeDtypeStruct((b, s, RWKV_CH), BF16),
                     jax.ShapeDtypeStruct((b, s, RWKV_CH), F32)]
    big = pltpu.VMEM((ts, RWKV_CH), F32)
    return pl.pallas_call(
        functools.partial(_rwkv_kernel, has_vres),
        grid=(b, s // ts),
        in_specs=in_specs,
        out_specs=out_specs,
        out_shape=out_shape,
        scratch_shapes=[pltpu.VMEM((ts + 8, RWKV_COLS), F32),
                        pltpu.VMEM((RWKV_CH // GROUP_W, GROUP_W, GROUP_W), F32),
                        big, big, big, big, big, big, big],
        compiler_params=pltpu.CompilerParams(
            dimension_semantics=("arbitrary", "arbitrary"), vmem_limit_bytes=VMEM_LIMIT),
        name="rwkv_branch",
    )(*args)


def _store_token_tiles(ref, x):
    rows = x.shape[0]
    for c in range(ROW_TILE):
        ref[pl.ds(c, rows, stride=ROW_TILE), :] = x[:, c * LANES:(c + 1) * LANES]


def _load_token_tiles(ref, first, rows):
    return jnp.concatenate(
        [ref[pl.ds(first * ROW_TILE + c, rows, stride=ROW_TILE), :] for c in range(ROW_TILE)], axis=1)


def _out_router_kernel(h_ref, yc_ref, yr_ref, wo_ref, g_ref, rwh_ref, rwl_ref, rb_ref,
                       h1_ref, u_ref, e_ref, gate_ref, loc_ref, cnt_ref, run_ref):
    @pl.when(pl.program_id(0) == 0)
    def _():
        run_ref[...] = jnp.zeros(run_ref.shape, F32)

    wo = wo_ref[...]
    h1 = (h_ref[...] + jnp.dot(yc_ref[...], wo[:CONV_CH], preferred_element_type=F32)
          + jnp.dot(yr_ref[...], wo[CONV_CH:], preferred_element_type=F32))
    h1_ref[...] = h1
    u = _rms(h1, g_ref[...])
    _store_token_tiles(u_ref, u)
    u_hi, u_lo = _split(u)
    rwh = rwh_ref[...]
    logits = (jnp.dot(u_hi, rwh, preferred_element_type=F32)
              + jnp.dot(u_lo, rwh, preferred_element_type=F32)
              + jnp.dot(u_hi, rwl_ref[...], preferred_element_type=F32)) + rb_ref[...]
    li = _iota2(logits.shape, 1)
    neg = jnp.float32(-jnp.inf)
    big = jnp.int32(ROUTER_COLS)
    gl = jnp.where(li < N_GROUPS, logits, neg)
    gmax = jnp.max(gl, axis=-1, keepdims=True)
    g_idx = jnp.min(jnp.where(gl == gmax, li, big), axis=-1, keepdims=True)
    g_p = 1.0 / jnp.sum(jnp.exp(gl - gmax), axis=-1, keepdims=True)
    e_lo = N_GROUPS + g_idx * EXPERTS_PER_GROUP
    el = jnp.where((li >= e_lo) & (li < e_lo + EXPERTS_PER_GROUP), logits, neg)
    v1 = jnp.max(el, axis=-1, keepdims=True)
    i1 = jnp.min(jnp.where(el == v1, li, big), axis=-1, keepdims=True)
    el2 = jnp.where(li == i1, neg, el)
    v2 = jnp.max(el2, axis=-1, keepdims=True)
    i2 = jnp.min(jnp.where(el2 == v2, li, big), axis=-1, keepdims=True)
    ex = jnp.exp(v2 - v1)
    w1 = g_p / (1.0 + ex)
    w2 = g_p * ex / (1.0 + ex)
    e_ref[...] = jnp.concatenate([i1, i2], axis=1) - N_GROUPS
    gate_ref[...] = jnp.concatenate([w1, w2], axis=1)

    tm = logits.shape[0]
    oh1 = li == i1 - N_GROUPS
    oh2 = li == i2 - N_GROUPS
    both = jnp.where(oh1 | oh2, 1.0, 0.0).astype(BF16)
    lower = (_iota2((tm, tm), 1) < _iota2((tm, tm), 0)).astype(BF16)
    before = jnp.dot(lower, both, preferred_element_type=F32) + run_ref[...]
    loc1 = jnp.sum(jnp.where(oh1, before, 0.0), axis=-1, keepdims=True)
    loc2 = jnp.sum(jnp.where(oh2, before, 0.0), axis=-1, keepdims=True)
    loc_ref[...] = jnp.concatenate([loc1, loc2], axis=1).astype(jnp.int32)
    run = run_ref[...] + jnp.sum(both.astype(F32), axis=0, keepdims=True)
    run_ref[...] = run
    cnt_ref[...] = run.astype(jnp.int32)


def _out_router(h, yc, yr, wo, g, rwh, rwl, rb):
    t, d = h.shape
    tm = min(512, t)
    full = lambda shape: pl.BlockSpec(shape, lambda i: (0,) * len(shape))
    rows = lambda width: pl.BlockSpec((tm, width), lambda i: (i, 0))
    return pl.pallas_call(
        _out_router_kernel,
        grid=(t // tm,),
        in_specs=[rows(d), rows(CONV_CH), rows(RWKV_CH), full((d, d)), full((1, d)),
                  full((d, ROUTER_COLS)), full((d, ROUTER_COLS)), full((1, ROUTER_COLS))],
        out_specs=[rows(d), pl.BlockSpec((tm * ROW_TILE, LANES), lambda i: (i, 0)),
                   rows(2), rows(2), rows(2), full((1, ROUTER_COLS))],
        out_shape=[jax.ShapeDtypeStruct((t, d), F32),
                   jax.ShapeDtypeStruct((t * ROW_TILE, LANES), F32),
                   jax.ShapeDtypeStruct((t, 2), jnp.int32), jax.ShapeDtypeStruct((t, 2), F32),
                   jax.ShapeDtypeStruct((t, 2), jnp.int32),
                   jax.ShapeDtypeStruct((1, ROUTER_COLS), jnp.int32)],
        scratch_shapes=[pltpu.VMEM((1, ROUTER_COLS), F32)],
        compiler_params=pltpu.CompilerParams(
            dimension_semantics=("arbitrary",), vmem_limit_bytes=VMEM_LIMIT),
        name="out_router",
    )(h, yc, yr, wo, g, rwh, rwl, rb)


def _tile_copy(src_hbm, src_tok, dst, dst_tok, sem):
    return pltpu.make_async_copy(
        src_hbm.at[pl.ds(pl.multiple_of(src_tok * ROW_TILE, ROW_TILE), ROW_TILE), :],
        dst.at[pl.ds(dst_tok * ROW_TILE, ROW_TILE), :],
        sem)


def _issue_tiles(src_hbm, idx_ref, idx_off, dst, sem, n, unrolled):
    if unrolled:
        for r in range(n):
            _tile_copy(src_hbm, idx_ref[0, idx_off + r], dst, r, sem).start()
    else:
        def body(r, carry):
            _tile_copy(src_hbm, idx_ref[0, idx_off + r], dst, r, sem).start()
            return carry
        lax.fori_loop(0, n, body, 0)


def _wait_tiles(src_hbm, dst, sem, n, unrolled):
    if unrolled:
        for r in range(n):
            _tile_copy(src_hbm, 0, dst, r, sem).wait()
    else:
        def body(r, carry):
            _tile_copy(src_hbm, 0, dst, r, sem).wait()
            return carry
        lax.fori_loop(0, n, body, 0)


def _expert_kernel(be_ref, cnt_ref, tok_ref, nxt_ref, u_hbm, wga_ref, wua_ref, wda_ref,
                   wgb_ref, wub_ref, wdb_ref, o_ref, xa, xb, sem):
    j = pl.program_id(0)
    n = MOE_BLOCK
    rows = n * ROW_TILE

    @pl.when(j == 0)
    def _():
        _issue_tiles(u_hbm, tok_ref, 0, xa, sem.at[0], n, False)

    def half(blk, cur, cur_sem, nxt, nxt_sem, nxt_idx, nxt_off, wg_ref, wu_ref, wd_ref, out_rows):
        @pl.when(cnt_ref[blk] > 0)
        def _():
            _wait_tiles(u_hbm, cur, cur_sem, n, True)
            _issue_tiles(u_hbm, nxt_idx, nxt_off, nxt, nxt_sem, n, True)
            x = _load_token_tiles(cur, 0, n).astype(BF16)
            hg = jnp.dot(x, wg_ref[...], preferred_element_type=F32)
            hu = jnp.dot(x, wu_ref[...], preferred_element_type=F32)
            hid = (hg * _sigmoid(hg) * hu).astype(BF16)
            _store_token_tiles(o_ref.at[out_rows], jnp.dot(hid, wd_ref[...], preferred_element_type=F32))

        @pl.when(cnt_ref[blk] <= 0)
        def _():
            o_ref[out_rows, :] = jnp.zeros((rows, LANES), o_ref.dtype)

        @pl.when((cnt_ref[blk] <= 0) & (blk > 0) & (cnt_ref[jnp.maximum(blk - 1, 0)] > 0))
        def _():
            _wait_tiles(u_hbm, cur, cur_sem, n, False)

    half(2 * j, xa, sem.at[0], xb, sem.at[1], tok_ref, n, wga_ref, wua_ref, wda_ref, pl.ds(0, rows))
    half(2 * j + 1, xb, sem.at[1], xa, sem.at[0], nxt_ref, 0, wgb_ref, wub_ref, wdb_ref, pl.ds(rows, rows))


def _expert_mlp(blk_e, blk_cnt, buf_tok, u_tiles, wg, wu, wd):
    n_blocks = blk_e.shape[0]
    steps = n_blocks // 2
    d = wg.shape[1]
    tok3 = buf_tok.reshape(steps, 1, 2 * MOE_BLOCK)
    w_in = lambda par: pl.BlockSpec((None, d, EXPERT_FF), lambda i, be, bc: (be[2 * i + par], 0, 0))
    w_out = lambda par: pl.BlockSpec((None, EXPERT_FF, d), lambda i, be, bc: (be[2 * i + par], 0, 0))
    grid_spec = pltpu.PrefetchScalarGridSpec(
        num_scalar_prefetch=2,
        grid=(steps,),
        in_specs=[
            pl.BlockSpec((None, 1, 2 * MOE_BLOCK), lambda i, be, bc: (i, 0, 0), memory_space=pltpu.SMEM),
            pl.BlockSpec((None, 1, 2 * MOE_BLOCK), lambda i, be, bc: (jnp.minimum(i + 1, steps - 1), 0, 0),
                         memory_space=pltpu.SMEM),
            pl.BlockSpec(memory_space=pl.ANY),
            w_in(0), w_in(0), w_out(0), w_in(1), w_in(1), w_out(1),
        ],
        out_specs=pl.BlockSpec((2 * MOE_BLOCK * ROW_TILE, LANES), lambda i, be, bc: (i, 0)),
        scratch_shapes=[pltpu.VMEM((MOE_BLOCK * ROW_TILE, LANES), F32),
                        pltpu.VMEM((MOE_BLOCK * ROW_TILE, LANES), F32),
                        pltpu.SemaphoreType.DMA((2,))],
    )
    return pl.pallas_call(
        _expert_kernel,
        grid_spec=grid_spec,
        out_shape=jax.ShapeDtypeStruct((n_blocks * MOE_BLOCK * ROW_TILE, LANES), F32),
        compiler_params=pltpu.CompilerParams(
            dimension_semantics=("arbitrary",), vmem_limit_bytes=VMEM_LIMIT),
        name="expert_mlp",
    )(blk_e, blk_cnt, tok3, tok3, u_tiles, wg, wu, wd, wg, wu, wd)


def _combine_kernel(final, pos_ref, nxt_ref, gate_ref, h_ref, p_ref, y_hbm, g_ref, wg_ref, wp_ref,
                    fg_ref, o_ref, ya, yb, sem):
    i = pl.program_id(0)
    last = pl.num_programs(0) - 1
    tm = h_ref.shape[0] // 2
    n = 2 * tm

    @pl.when(i == 0)
    def _():
        _issue_tiles(y_hbm, pos_ref, 0, ya, sem.at[0], n, False)

    def half(cur, cur_sem, nxt, nxt_sem, nxt_idx, nxt_off, rows):
        _wait_tiles(y_hbm, cur, cur_sem, n, True)
        _issue_tiles(y_hbm, nxt_idx, nxt_off, nxt, nxt_sem, n, True)
        gate = gate_ref[rows, :]
        h2 = (h_ref[rows, :] + gate[:, 0:1] * _load_token_tiles(cur, 0, tm)
              + gate[:, 1:2] * _load_token_tiles(cur, tm, tm))
        u = _rms(h2, g_ref[...]).astype(BF16)
        pgate = _sigmoid(jnp.dot(u, wg_ref[...], preferred_element_type=F32))
        pp = jnp.dot(p_ref[rows, :].astype(BF16), wp_ref[...], preferred_element_type=F32)
        h3 = h2 + pgate * pp
        if final:
            h3 = _rms(h3, fg_ref[...])
        o_ref[rows, :] = h3

    half(ya, sem.at[0], yb, sem.at[1], pos_ref, n, pl.ds(0, tm))
    half(yb, sem.at[1], ya, sem.at[0], nxt_ref, 0, pl.ds(tm, tm))

    @pl.when(i == last)
    def _():
        _wait_tiles(y_hbm, ya, sem.at[0], n, False)


def _combine_ple(final, pos, gate_w, h1, p, y_tiles, g, wg, wp, fg):
    t, d = h1.shape
    tm = min(256, t // 2)
    nb = t // (2 * tm)
    pos_blk = jnp.concatenate([pos[:, 0].reshape(2 * nb, 1, tm), pos[:, 1].reshape(2 * nb, 1, tm)], axis=2)
    pos_blk = pos_blk.reshape(nb, 1, 4 * tm)
    full = lambda shape: pl.BlockSpec(shape, lambda i: (0,) * len(shape))
    rows = lambda width: pl.BlockSpec((2 * tm, width), lambda i: (i, 0))
    return pl.pallas_call(
        functools.partial(_combine_kernel, final),
        grid=(nb,),
        in_specs=[
            pl.BlockSpec((None, 1, 4 * tm), lambda i: (i, 0, 0), memory_space=pltpu.SMEM),
            pl.BlockSpec((None, 1, 4 * tm), lambda i: (jnp.minimum(i + 1, nb - 1), 0, 0),
                         memory_space=pltpu.SMEM),
            rows(2), rows(d), rows(PLE_DIM),
            pl.BlockSpec(memory_space=pl.ANY),
            full((1, d)), full((d, d)), full((PLE_DIM, d)), full((1, d)),
        ],
        out_specs=rows(d),
        out_shape=jax.ShapeDtypeStruct((t, d), F32),
        scratch_shapes=[pltpu.VMEM((2 * tm * ROW_TILE, LANES), F32),
                        pltpu.VMEM((2 * tm * ROW_TILE, LANES), F32),
                        pltpu.SemaphoreType.DMA((2,))],
        compiler_params=pltpu.CompilerParams(
            dimension_semantics=("arbitrary",), vmem_limit_bytes=VMEM_LIMIT),
        name="combine_ple",
    )(pos_blk, pos_blk, gate_w, h1, p, y_tiles, g, wg, wp, fg)


def _route_plan(e_idx, local, counts):
    t = e_idx.shape[0]
    a = t * 2
    n_blocks = -(-a // MOE_BLOCK) + N_EXPERTS
    n_blocks += n_blocks % 2
    starts = jnp.cumsum(counts) - counts
    pcounts = (counts + MOE_BLOCK - 1) // MOE_BLOCK * MOE_BLOCK
    pend = jnp.cumsum(pcounts)
    pstarts = pend - pcounts
    onehot = e_idx[..., None] == jnp.arange(N_EXPERTS, dtype=jnp.int32)
    pos = local + jnp.sum(jnp.where(onehot, pstarts, 0), axis=-1)
    blk_start = jnp.arange(n_blocks, dtype=jnp.int32) * MOE_BLOCK
    blk_e = jnp.minimum(jnp.searchsorted(pend, blk_start, side='right'), N_EXPERTS - 1).astype(jnp.int32)
    blk_off = blk_start - pstarts[blk_e]
    blk_cnt = jnp.clip(counts[blk_e] - blk_off, 0, MOE_BLOCK).astype(jnp.int32)
    order = jnp.argsort(e_idx.reshape(-1)).astype(jnp.int32)
    r = jnp.arange(MOE_BLOCK, dtype=jnp.int32)[None, :]
    src = (starts[blk_e] + blk_off)[:, None] + r
    buf_tok = jnp.where(r < blk_cnt[:, None], order[jnp.clip(src, 0, a - 1)] // 2, 0).astype(jnp.int32)
    return blk_e, blk_cnt, buf_tok, pos.astype(jnp.int32)


def kernel(x, p, mix_norm, w_in, conv_w, conv_b, conv_ln_g, conv_ln_b, rwkv_mu, rwkv_w0, rwkv_w2, rwkv_a0, rwkv_a2, rwkv_g2, rwkv_kk, rwkv_ka, rwkv_rk, rwkv_gn_g, rwkv_gn_b, rwkv_v0, rwkv_v1, rwkv_v2, w_out, ffn_norm, router_group_w, router_group_b, router_expert_w, router_expert_b, expert_w_gate, expert_w_up, expert_w_down, ple_norm, ple_gate_w, ple_proj_w, final_norm):
    b, s, d = x.shape
    depth = w_in.shape[0]
    t = b * s
    n_lora = LORA_W + LORA_A + LORA_G
    row = lambda v: v.reshape(1, -1).astype(F32)

    h = x
    v_first = None
    for i in range(depth):
        w_conv = w_in[i, :, :2 * CONV_CH].astype(BF16)
        w_rwkv = jnp.pad(w_in[i, :, 2 * CONV_CH:], ((0, 0), (0, LORA_PAD - n_lora))).astype(BF16)
        mu = jnp.pad(rwkv_mu[i], (0, LORA_PAD - n_lora)).reshape(1, -1)
        cw = jnp.pad(conv_w[i], ((0, CONV_HALO - CONV_WIDTH), (0, 0)))
        yc = _conv_branch(h, row(mix_norm[i]), w_conv, cw, row(conv_b[i]), row(conv_ln_g[i]),
                          row(conv_ln_b[i]))

        wl = jnp.zeros((LORA_PAD, 3 * RWKV_CH), F32)
        wl = wl.at[0:LORA_W, 0:RWKV_CH].set(rwkv_w2[i])
        wl = wl.at[LORA_W:LORA_W + LORA_A, RWKV_CH:2 * RWKV_CH].set(rwkv_a2[i])
        wl = wl.at[LORA_W + LORA_A:n_lora, 2 * RWKV_CH:].set(rwkv_g2[i])
        if i == 0:
            vres = None
        else:
            v1 = jnp.pad(rwkv_v1[i - 1], ((0, 0), (0, LORA_V_PAD - rwkv_v1.shape[2]))).astype(BF16)
            v2 = jnp.pad(rwkv_v2[i - 1], ((0, LORA_V_PAD - rwkv_v2.shape[1]), (0, 0))).astype(BF16)
            vres = (v_first, row(rwkv_v0[i - 1]), v1, v2)
        res = _rwkv_branch(h, row(mix_norm[i]), w_rwkv, mu, row(rwkv_w0[i]), row(rwkv_a0[i]),
                           wl.astype(BF16), row(rwkv_kk[i]), row(rwkv_ka[i]), row(rwkv_rk[i]),
                           row(rwkv_gn_g[i]), row(rwkv_gn_b[i]), vres)
        if i == 0:
            yr, v_first = res
        else:
            yr = res

        rw = jnp.concatenate(
            [router_group_w[i]] + [router_expert_w[i, gidx] for gidx in range(N_GROUPS)], axis=1)
        rw = jnp.pad(rw, ((0, 0), (0, ROUTER_COLS - rw.shape[1])))
        rw_hi = rw.astype(BF16)
        rw_lo = (rw - rw_hi.astype(F32)).astype(BF16)
        rb = jnp.concatenate([router_group_b[i], router_expert_b[i].reshape(-1)])
        rb = jnp.pad(rb, (0, ROUTER_COLS - rb.shape[0])).reshape(1, -1)
        h1, u_tiles, e_idx, gate_w, local, counts = _out_router(
            h.reshape(t, d), yc.reshape(t, CONV_CH), yr.reshape(t, RWKV_CH),
            w_out[i].astype(BF16), row(ffn_norm[i]), rw_hi, rw_lo, rb)

        blk_e, blk_cnt, buf_tok, pos = _route_plan(e_idx, local, counts[0, :N_EXPERTS])
        y_tiles = _expert_mlp(blk_e, blk_cnt, buf_tok, u_tiles,
                              expert_w_gate[i].astype(BF16), expert_w_up[i].astype(BF16),
                              expert_w_down[i].astype(BF16))
        h = _combine_ple(i == depth - 1, pos, gate_w, h1, p[i].reshape(t, PLE_DIM), y_tiles,
                         row(ple_norm[i]), ple_gate_w[i].astype(BF16), ple_proj_w[i].astype(BF16),
                         row(final_norm)).reshape(b, s, d)
    return h
```

```python
import functools

import jax
import jax.numpy as jnp
from jax import lax
from jax.experimental import pallas as pl
from jax.experimental.pallas import tpu as pltpu

F32 = jnp.float32
BF16 = jnp.bfloat16

D_MODEL = 1024
CONV_CH = 512
CONV_WIDTH = 31
CONV_HALO = 32
RWKV_CH = 512
RWKV_HEAD = 64
LORA_W = 32
LORA_A = 32
LORA_G = 96
LORA_PAD = 256
RWKV_COLS = 3 * RWKV_CH + LORA_PAD
LORA_V_PAD = 128
N_GROUPS = 4
EXPERTS_PER_GROUP = 8
N_EXPERTS = N_GROUPS * EXPERTS_PER_GROUP
ROUTER_COLS = 128
EXPERT_FF = 512
MOE_BLOCK = 512
LANES = 128
ROW_TILE = 8
PLE_DIM = 256
NORM_EPS = 1e-6
LN_EPS = 1e-5
GN_EPS = 64e-5

CHUNK = 64
GROUP_HEADS = 4
GROUP_W = GROUP_HEADS * RWKV_HEAD
INV_BLOCK = 16

VMEM_LIMIT = 56 * 1024 * 1024
GATHER_PRIORITY = 1


def _sigmoid(x):
    return 1.0 / (1.0 + jnp.exp(-x))


def _rms(x, g):
    ms = jnp.mean(x * x, axis=-1, keepdims=True)
    return x * lax.rsqrt(ms + NORM_EPS) * g


def _dot(a, b):
    return jnp.dot(a.astype(BF16), b.astype(BF16), preferred_element_type=F32)


def _dot_nt(a, b):
    return lax.dot_general(a.astype(BF16), b.astype(BF16), (((1,), (1,)), ((), ())),
                           preferred_element_type=F32)


def _dot_tn(a, b):
    return lax.dot_general(a.astype(BF16), b.astype(BF16), (((0,), (0,)), ((), ())),
                           preferred_element_type=F32)


def _split(x):
    hi = x.astype(BF16)
    lo = (x - hi.astype(F32)).astype(BF16)
    return hi, lo


def _iota2(shape, dim):
    return lax.broadcasted_iota(jnp.int32, shape, dim)


CONV_ROWS = 64


def _conv_kernel(h_ref, g_ref, w_ref, cw_ref, cb_ref, lg_ref, lb_ref, o_ref, ubuf, shifted):
    ts = h_ref.shape[0]
    s = pl.program_id(1)

    @pl.when(s == 0)
    def _():
        ubuf[0:CONV_HALO, :] = jnp.zeros((CONV_HALO, CONV_CH), F32)

    @pl.when(s > 0)
    def _():
        ubuf[0:CONV_HALO, :] = ubuf[ts:ts + CONV_HALO, :]

    u = _rms(h_ref[...], g_ref[...]).astype(BF16)
    z = jnp.dot(u, w_ref[...], preferred_element_type=F32)
    ubuf[CONV_HALO:CONV_HALO + ts, :] = z[:, :CONV_CH] * _sigmoid(z[:, CONV_CH:])

    span = ts + CONV_HALO - ROW_TILE
    for ph in range(1, ROW_TILE):
        shifted[ph - 1, 0:span, :] = ubuf[ph:ph + span, :]

    base = CONV_HALO - (CONV_WIDTH - 1)
    for c in range(ts // CONV_ROWS):
        r0 = c * CONV_ROWS
        acc = jnp.broadcast_to(cb_ref[...], (CONV_ROWS, CONV_CH))
        for j in range(CONV_WIDTH):
            ph, lo = (base + j) % ROW_TILE, r0 + (base + j) // ROW_TILE * ROW_TILE
            src = ubuf[lo:lo + CONV_ROWS, :] if ph == 0 else shifted[ph - 1, lo:lo + CONV_ROWS, :]
            acc = acc + cw_ref[j:j + 1, :] * src
        mean = jnp.mean(acc, axis=-1, keepdims=True)
        cen = acc - mean
        var = jnp.mean(cen * cen, axis=-1, keepdims=True)
        un = cen * lax.rsqrt(var + LN_EPS) * lg_ref[...] + lb_ref[...]
        o_ref[r0:r0 + CONV_ROWS, :] = (un * _sigmoid(un)).astype(o_ref.dtype)


def _conv_branch(h, g, w_conv, cw, cb, lg, lb):
    b, s, d = h.shape
    ts = min(512, s)
    full = lambda shape: pl.BlockSpec(shape, lambda i, j: (0,) * len(shape))
    return pl.pallas_call(
        _conv_kernel,
        grid=(b, s // ts),
        in_specs=[
            pl.BlockSpec((None, ts, d), lambda i, j: (i, j, 0)),
            full((1, d)),
            full((d, 2 * CONV_CH)),
            full((CONV_HALO, CONV_CH)),
            full((1, CONV_CH)),
            full((1, CONV_CH)),
            full((1, CONV_CH)),
        ],
        out_specs=pl.BlockSpec((None, ts, CONV_CH), lambda i, j: (i, j, 0)),
        out_shape=jax.ShapeDtypeStruct((b, s, CONV_CH), BF16),
        scratch_shapes=[pltpu.VMEM((ts + CONV_HALO, CONV_CH), F32),
                        pltpu.VMEM((ROW_TILE - 1, ts + CONV_HALO, CONV_CH), F32)],
        compiler_params=pltpu.CompilerParams(
            dimension_semantics=("arbitrary", "arbitrary"), vmem_limit_bytes=VMEM_LIMIT),
        name="conv_branch",
    )(h, g, w_conv, cw, cb, lg, lb)


def _bd_rows(y, n, w):
    r = y.shape[0]
    yt = jnp.concatenate([y] * n, axis=0)
    keep = (_iota2(yt.shape, 0) // r) == (_iota2(yt.shape, 1) // w)
    return jnp.where(keep, yt, jnp.zeros_like(yt))


def _pp(x, y):
    n = x.shape[1] // y.shape[0]
    return jnp.dot(x.astype(BF16), _bd_rows(y.astype(BF16), n, y.shape[1] // n),
                   preferred_element_type=F32)


def _tri_inv_all(a_list, eye, blk):
    a_d = [jnp.where(blk, a, 0.0) for a in a_list]
    a_o = [a - ad for a, ad in zip(a_list, a_d)]
    x = a_d
    d = [eye + xi for xi in x]
    for _ in range(3):
        x = [_pp(xi, xi) for xi in x]
        d = [di + _pp(di, xi) for di, xi in zip(d, x)]
    n1 = [_pp(di, ao) for di, ao in zip(d, a_o)]
    n2 = [_pp(n, n) for n in n1]
    e = [di + _pp(n, di) for di, n in zip(d, n1)]
    return [ei + _pp(n, ei) for ei, n in zip(e, n2)]


def _dot_exact_lhs(tri_bf16, x):
    hi, lo = _split(x)
    return (jnp.dot(tri_bf16, hi, preferred_element_type=F32)
            + jnp.dot(tri_bf16, lo, preferred_element_type=F32))


def _rwkv_kernel(has_vres, *refs):
    it = iter(refs)
    h_ref, g_ref, w_ref, mu_ref = next(it), next(it), next(it), next(it)
    w0_ref, a0_ref, wl_ref = next(it), next(it), next(it)
    kk_ref, ka_ref, rk_ref, gng_ref, gnb_ref = next(it), next(it), next(it), next(it), next(it)
    if has_vres:
        vf_ref, v0_ref, v1_ref, v2_ref = next(it), next(it), next(it), next(it)
    y_ref = next(it)
    vfo_ref = None if has_vres else next(it)
    zbuf, st_ref, r_s, k_s, v_s, lw_s, a_s, b_s, y_s = (next(it) for _ in range(9))

    ts = h_ref.shape[0]
    s = pl.program_id(1)
    ch = RWKV_CH
    n_grp = ch // GROUP_W

    @pl.when(s == 0)
    def _():
        zbuf[0:8, :] = jnp.zeros((8, RWKV_COLS), F32)
        st_ref[...] = jnp.zeros(st_ref.shape, F32)

    @pl.when(s > 0)
    def _():
        zbuf[0:8, :] = zbuf[ts:ts + 8, :]

    u = _rms(h_ref[...], g_ref[...]).astype(BF16)
    z = jnp.dot(u, w_ref[...], preferred_element_type=F32)
    zbuf[8:8 + ts, :] = z
    zprev = zbuf[7:7 + ts, :]
    zs = z + (zprev - z) * mu_ref[...]

    r = zs[:, 0:ch]
    k = zs[:, ch:2 * ch]
    v = zs[:, 2 * ch:3 * ch]
    lo = zs[:, 3 * ch:3 * ch + LORA_PAD]
    li = _iota2(lo.shape, 1)
    act = jnp.where(li < LORA_W, jnp.tanh(lo),
                    jnp.where(li < LORA_W + LORA_A, lo, _sigmoid(lo)))
    lora = _dot(act, wl_ref[...])
    w_pre = w0_ref[...] + lora[:, 0:ch]
    a = _sigmoid(a0_ref[...] + lora[:, ch:2 * ch])
    g = lora[:, 2 * ch:3 * ch]
    nw = -w_pre
    softplus = jnp.maximum(nw, 0.0) + jnp.log(1.0 + jnp.exp(-jnp.abs(nw)))
    logw = -jnp.exp(-softplus - 0.5)

    if has_vres:
        vv = _dot(_dot(v, v1_ref[...]), v2_ref[...])
        v = v + (vf_ref[...] - v) * _sigmoid(v0_ref[...] + vv)
    else:
        vfo_ref[...] = v

    ones_bd = ((_iota2((ch, ch), 0) // RWKV_HEAD) == (_iota2((ch, ch), 1) // RWKV_HEAD)).astype(BF16)
    kk = k * kk_ref[...]
    ss = _dot(kk * kk, ones_bd)
    kk = kk / jnp.maximum(jnp.sqrt(ss), 1e-12)
    k = k * (1.0 + (a - 1.0) * ka_ref[...])

    r_s[...] = r
    k_s[...] = k
    v_s[...] = v
    lw_s[...] = logw
    a_s[...] = -kk
    b_s[...] = kk * a

    c = CHUNK
    gw = GROUP_W
    row = _iota2((c, gw), 0)
    col = _iota2((c, gw), 1) % c
    strict = col < row
    incl = col <= row
    eye = (col == row).astype(F32)
    blk = (col // INV_BLOCK) == (row // INV_BLOCK)
    tri = (_iota2((c, c), 1) <= _iota2((c, c), 0)).astype(BF16)
    st_keep = (_iota2((gw, gw), 0) // RWKV_HEAD) == (_iota2((gw, gw), 1) // RWKV_HEAD)

    n_chunks = ts // c
    prob = [(ci, gi) for ci in range(n_chunks) for gi in range(n_grp)]
    at, rt, bt, kt, bh, kh, vc, p_all = [], [], [], [], [], [], [], []
    for ci in range(n_chunks):
        rows = slice(ci * c, (ci + 1) * c)
        lw = lw_s[rows, :]
        cum = _dot_exact_lhs(tri, lw)
        tot = cum[c - 1:c, :]
        p_inv = jnp.exp(-cum)
        p_end = jnp.exp(tot - cum)
        at.append((a_s[rows, :] * jnp.exp(cum - lw)).astype(BF16))
        rt.append((r_s[rows, :] * jnp.exp(cum)).astype(BF16))
        bt.append((b_s[rows, :] * p_inv).astype(BF16))
        kt.append((k_s[rows, :] * p_inv).astype(BF16))
        bh.append((b_s[rows, :] * p_end).astype(BF16))
        kh.append((k_s[rows, :] * p_end).astype(BF16))
        vc.append(v_s[rows, :].astype(BF16))
        p_all.append(jnp.exp(tot))
    grp = lambda lst: [lst[ci][:, gi * gw:(gi + 1) * gw] for ci, gi in prob]
    at, rt, bt, kt, bh, kh, vg = grp(at), grp(rt), grp(bt), grp(kt), grp(bh), grp(kh), grp(vc)
    ar = [jnp.concatenate([a, r], axis=0) for a, r in zip(at, rt)]
    m_b = [_dot_nt(x, _bd_rows(y, GROUP_HEADS, RWKV_HEAD)) for x, y in zip(ar, bt)]
    m_k = [_dot_nt(x, _bd_rows(y, GROUP_HEADS, RWKV_HEAD)) for x, y in zip(ar, kt)]
    a_ak = [jnp.where(strict, m[:c], 0.0) for m in m_k]
    a_rb = [jnp.where(incl, m[c:], 0.0) for m in m_b]
    a_rk = [jnp.where(incl, m[c:], 0.0) for m in m_k]
    t_inv = _tri_inv_all([jnp.where(strict, m[:c], 0.0) for m in m_b], eye, blk)
    akv = [_pp(a, v) for a, v in zip(a_ak, vg)]
    wt = [_pp(t, a) for t, a in zip(t_inv, at)]
    ut = [_pp(t, x) for t, x in zip(t_inv, akv)]
    r2 = [r.astype(F32) + _pp(a, w) for r, a, w in zip(rt, a_rb, wt)]
    y0 = [_pp(a, u) + _pp(b, v) for a, u, b, v in zip(a_rb, ut, a_rk, vg)]
    g_mat = [jnp.where(st_keep, _dot_tn(w, b), 0.0).astype(BF16) for w, b in zip(wt, bh)]
    h_mat = [jnp.where(st_keep, _dot_tn(jnp.concatenate([u.astype(BF16), v], axis=0),
                                        jnp.concatenate([b, k], axis=0)), 0.0)
             for u, v, b, k in zip(ut, vg, bh, kh)]
    st = [st_ref[gi] for gi in range(n_grp)]
    for ci in range(n_chunks):
        ys = []
        for gi in range(n_grp):
            q = ci * n_grp + gi
            st_b = st[gi].astype(BF16)
            ys.append(_dot_nt(r2[q], st_b) + y0[q])
            st[gi] = (st[gi] * p_all[ci][:, gi * gw:(gi + 1) * gw]
                      + jnp.dot(st_b, g_mat[q], preferred_element_type=F32) + h_mat[q])
        y_s[ci * c:(ci + 1) * c, :] = jnp.concatenate(ys, axis=1)
    for gi in range(n_grp):
        st_ref[gi] = st[gi]

    y = y_s[...]
    mean = _dot(y, ones_bd) * (1.0 / RWKV_HEAD)
    cen = y - mean
    var = _dot(cen * cen, ones_bd) * (1.0 / RWKV_HEAD)
    yn = cen * lax.rsqrt(var + GN_EPS) * gng_ref[...] + gnb_ref[...]
    bonus = _dot(r * k * rk_ref[...], ones_bd)
    y_ref[...] = ((yn + bonus * v) * g).astype(y_ref.dtype)


def _rwkv_branch(h, g, w_rwkv, mu, w0, a0, wl, kk, ka, rk, gng, gnb, vres):
    b, s, d = h.shape
    ts = min(256, s)
    has_vres = vres is not None
    full = lambda shape: pl.BlockSpec(shape, lambda i, j: (0,) * len(shape))
    row_blk = lambda width: pl.BlockSpec((None, ts, width), lambda i, j: (i, j, 0))
    vec = full((1, RWKV_CH))
    in_specs = [row_blk(d), full((1, d)), full((d, RWKV_COLS)), full((1, RWKV_COLS)),
                vec, vec, full((LORA_PAD, 3 * RWKV_CH)), vec, vec, vec, vec, vec]
    args = [h, g, w_rwkv, mu, w0, a0, wl, kk, ka, rk, gng, gnb]
    if has_vres:
        vf, v0, v1, v2 = vres
        in_specs += [row_blk(RWKV_CH), vec, full((RWKV_CH, LORA_V_PAD)), full((LORA_V_PAD, RWKV_CH))]
        args += [vf, v0, v1, v2]
        out_specs = row_blk(RWKV_CH)
        out_shape = jax.ShapeDtypeStruct((b, s, RWKV_CH), BF16)
    else:
        out_specs = [row_blk(RWKV_CH), row_blk(RWKV_CH)]
        out_shape = [jax.ShapeDtypeStruct((b, s, RWKV_CH), BF16),
                     jax.ShapeDtypeStruct((b, s, RWKV_CH), F32)]
    big = pltpu.VMEM((ts, RWKV_CH), F32)
    return pl.pallas_call(
        functools.partial(_rwkv_kernel, has_vres),
        grid=(b, s // ts),
        in_specs=in_specs,
        out_specs=out_specs,
        out_shape=out_shape,
        scratch_shapes=[pltpu.VMEM((ts + 8, RWKV_COLS), F32),
                        pltpu.VMEM((RWKV_CH // GROUP_W, GROUP_W, GROUP_W), F32),
                        big, big, big, big, big, big, big],
        compiler_params=pltpu.CompilerParams(
            dimension_semantics=("arbitrary", "arbitrary"), vmem_limit_bytes=VMEM_LIMIT),
        name="rwkv_branch",
    )(*args)


def _store_token_tiles(ref, x):
    rows = x.shape[0]
    for c in range(ROW_TILE):
        ref[pl.ds(c, rows, stride=ROW_TILE), :] = x[:, c * LANES:(c + 1) * LANES]


def _load_token_tiles(ref, first, rows):
    return jnp.concatenate(
        [ref[pl.ds(first * ROW_TILE + c, rows, stride=ROW_TILE), :] for c in range(ROW_TILE)], axis=1)


def _out_router_kernel(h_ref, yc_ref, yr_ref, wo_ref, g_ref, rwh_ref, rwl_ref, rb_ref,
                       h1_ref, u_ref, e_ref, gate_ref, loc_ref, cnt_ref, run_ref):
    @pl.when(pl.program_id(0) == 0)
    def _():
        run_ref[...] = jnp.zeros(run_ref.shape, F32)

    wo = wo_ref[...]
    h1 = (h_ref[...] + jnp.dot(yc_ref[...], wo[:CONV_CH], preferred_element_type=F32)
          + jnp.dot(yr_ref[...], wo[CONV_CH:], preferred_element_type=F32))
    h1_ref[...] = h1
    u = _rms(h1, g_ref[...])
    _store_token_tiles(u_ref, u)
    u_hi, u_lo = _split(u)
    rwh = rwh_ref[...]
    logits = (jnp.dot(u_hi, rwh, preferred_element_type=F32)
              + jnp.dot(u_lo, rwh, preferred_element_type=F32)
              + jnp.dot(u_hi, rwl_ref[...], preferred_element_type=F32)) + rb_ref[...]
    li = _iota2(logits.shape, 1)
    neg = jnp.float32(-jnp.inf)
    big = jnp.int32(ROUTER_COLS)
    gl = jnp.where(li < N_GROUPS, logits, neg)
    gmax = jnp.max(gl, axis=-1, keepdims=True)
    g_idx = jnp.min(jnp.where(gl == gmax, li, big), axis=-1, keepdims=True)
    g_p = 1.0 / jnp.sum(jnp.exp(gl - gmax), axis=-1, keepdims=True)
    e_lo = N_GROUPS + g_idx * EXPERTS_PER_GROUP
    el = jnp.where((li >= e_lo) & (li < e_lo + EXPERTS_PER_GROUP), logits, neg)
    v1 = jnp.max(el, axis=-1, keepdims=True)
    i1 = jnp.min(jnp.where(el == v1, li, big), axis=-1, keepdims=True)
    el2 = jnp.where(li == i1, neg, el)
    v2 = jnp.max(el2, axis=-1, keepdims=True)
    i2 = jnp.min(jnp.where(el2 == v2, li, big), axis=-1, keepdims=True)
    ex = jnp.exp(v2 - v1)
    w1 = g_p / (1.0 + ex)
    w2 = g_p * ex / (1.0 + ex)
    e_ref[...] = jnp.concatenate([i1, i2], axis=1) - N_GROUPS
    gate_ref[...] = jnp.concatenate([w1, w2], axis=1)

    tm = logits.shape[0]
    oh1 = li == i1 - N_GROUPS
    oh2 = li == i2 - N_GROUPS
    both = jnp.where(oh1 | oh2, 1.0, 0.0).astype(BF16)
    lower = (_iota2((tm, tm), 1) < _iota2((tm, tm), 0)).astype(BF16)
    before = jnp.dot(lower, both, preferred_element_type=F32) + run_ref[...]
    loc1 = jnp.sum(jnp.where(oh1, before, 0.0), axis=-1, keepdims=True)
    loc2 = jnp.sum(jnp.where(oh2, before, 0.0), axis=-1, keepdims=True)
    loc_ref[...] = jnp.concatenate([loc1, loc2], axis=1).astype(jnp.int32)
    run = run_ref[...] + jnp.sum(both.astype(F32), axis=0, keepdims=True)
    run_ref[...] = run
    cnt_ref[...] = run.astype(jnp.int32)


def _out_router(h, yc, yr, wo, g, rwh, rwl, rb):
    t, d = h.shape
    tm = min(512, t)
    full = lambda shape: pl.BlockSpec(shape, lambda i: (0,) * len(shape))
    rows = lambda width: pl.BlockSpec((tm, width), lambda i: (i, 0))
    return pl.pallas_call(
        _out_router_kernel,
        grid=(t // tm,),
        in_specs=[rows(d), rows(CONV_CH), rows(RWKV_CH), full((d, d)), full((1, d)),
                  full((d, ROUTER_COLS)), full((d, ROUTER_COLS)), full((1, ROUTER_COLS))],
        out_specs=[rows(d), pl.BlockSpec((tm * ROW_TILE, LANES), lambda i: (i, 0)),
                   rows(2), rows(2), rows(2), full((1, ROUTER_COLS))],
        out_shape=[jax.ShapeDtypeStruct((t, d), F32),
                   jax.ShapeDtypeStruct((t * ROW_TILE, LANES), F32),
                   jax.ShapeDtypeStruct((t, 2), jnp.int32), jax.ShapeDtypeStruct((t, 2), F32),
                   jax.ShapeDtypeStruct((t, 2), jnp.int32),
                   jax.ShapeDtypeStruct((1, ROUTER_COLS), jnp.int32)],
        scratch_shapes=[pltpu.VMEM((1, ROUTER_COLS), F32)],
        compiler_params=pltpu.CompilerParams(
            dimension_semantics=("arbitrary",), vmem_limit_bytes=VMEM_LIMIT),
        name="out_router",
    )(h, yc, yr, wo, g, rwh, rwl, rb)


def _tile_copy(src_hbm, src_tok, dst, dst_tok, sem):
    return pltpu.make_async_copy(
        src_hbm.at[pl.ds(pl.multiple_of(src_tok * ROW_TILE, ROW_TILE), ROW_TILE), :],
        dst.at[pl.ds(dst_tok * ROW_TILE, ROW_TILE), :],
        sem)


def _issue_tiles(src_hbm, idx_ref, idx_off, dst, sem, n, unrolled):
    if unrolled:
        for r in range(n):
            _tile_copy(src_hbm, idx_ref[0, idx_off + r], dst, r, sem).start(priority=GATHER_PRIORITY)
    else:
        def body(r, carry):
            _tile_copy(src_hbm, idx_ref[0, idx_off + r], dst, r, sem).start(priority=GATHER_PRIORITY)
            return carry
        lax.fori_loop(0, n, body, 0)


def _wait_tiles(src_hbm, dst, sem, n, unrolled):
    if unrolled:
        for r in range(n):
            _tile_copy(src_hbm, 0, dst, r, sem).wait()
    else:
        def body(r, carry):
            _tile_copy(src_hbm, 0, dst, r, sem).wait()
            return carry
        lax.fori_loop(0, n, body, 0)


def _expert_kernel(be_ref, cnt_ref, tok_ref, nxt_ref, u_hbm, wga_ref, wua_ref, wda_ref,
                   wgb_ref, wub_ref, wdb_ref, o_ref, xa, xb, sem):
    j = pl.program_id(0)
    n = MOE_BLOCK
    rows = n * ROW_TILE

    @pl.when(j == 0)
    def _():
        _issue_tiles(u_hbm, tok_ref, 0, xa, sem.at[0], n, False)

    def half(blk, cur, cur_sem, nxt, nxt_sem, nxt_idx, nxt_off, wg_ref, wu_ref, wd_ref, out_rows):
        @pl.when(cnt_ref[blk] > 0)
        def _():
            _wait_tiles(u_hbm, cur, cur_sem, n, True)
            _issue_tiles(u_hbm, nxt_idx, nxt_off, nxt, nxt_sem, n, True)
            x = _load_token_tiles(cur, 0, n).astype(BF16)
            hg = jnp.dot(x, wg_ref[...], preferred_element_type=F32)
            hu = jnp.dot(x, wu_ref[...], preferred_element_type=F32)
            hid = (hg * _sigmoid(hg) * hu).astype(BF16)
            _store_token_tiles(o_ref.at[out_rows], jnp.dot(hid, wd_ref[...], preferred_element_type=F32))

        @pl.when(cnt_ref[blk] <= 0)
        def _():
            o_ref[out_rows, :] = jnp.zeros((rows, LANES), o_ref.dtype)

        @pl.when((cnt_ref[blk] <= 0) & (blk > 0) & (cnt_ref[jnp.maximum(blk - 1, 0)] > 0))
        def _():
            _wait_tiles(u_hbm, cur, cur_sem, n, False)

    half(2 * j, xa, sem.at[0], xb, sem.at[1], tok_ref, n, wga_ref, wua_ref, wda_ref, pl.ds(0, rows))
    half(2 * j + 1, xb, sem.at[1], xa, sem.at[0], nxt_ref, 0, wgb_ref, wub_ref, wdb_ref, pl.ds(rows, rows))


def _expert_mlp(blk_e, blk_cnt, buf_tok, u_tiles, wg, wu, wd):
    n_blocks = blk_e.shape[0]
    steps = n_blocks // 2
    d = wg.shape[1]
    tok3 = buf_tok.reshape(steps, 1, 2 * MOE_BLOCK)
    w_in = lambda par: pl.BlockSpec((None, d, EXPERT_FF), lambda i, be, bc: (be[2 * i + par], 0, 0))
    w_out = lambda par: pl.BlockSpec((None, EXPERT_FF, d), lambda i, be, bc: (be[2 * i + par], 0, 0))
    grid_spec = pltpu.PrefetchScalarGridSpec(
        num_scalar_prefetch=2,
        grid=(steps,),
        in_specs=[
            pl.BlockSpec((None, 1, 2 * MOE_BLOCK), lambda i, be, bc: (i, 0, 0), memory_space=pltpu.SMEM),
            pl.BlockSpec((None, 1, 2 * MOE_BLOCK), lambda i, be, bc: (jnp.minimum(i + 1, steps - 1), 0, 0),
                         memory_space=pltpu.SMEM),
            pl.BlockSpec(memory_space=pl.ANY),
            w_in(0), w_in(0), w_out(0), w_in(1), w_in(1), w_out(1),
        ],
        out_specs=pl.BlockSpec((2 * MOE_BLOCK * ROW_TILE, LANES), lambda i, be, bc: (i, 0)),
        scratch_shapes=[pltpu.VMEM((MOE_BLOCK * ROW_TILE, LANES), F32),
                        pltpu.VMEM((MOE_BLOCK * ROW_TILE, LANES), F32),
                        pltpu.SemaphoreType.DMA((2,))],
    )
    return pl.pallas_call(
        _expert_kernel,
        grid_spec=grid_spec,
        out_shape=jax.ShapeDtypeStruct((n_blocks * MOE_BLOCK * ROW_TILE, LANES), F32),
        compiler_params=pltpu.CompilerParams(
            dimension_semantics=("arbitrary",), vmem_limit_bytes=VMEM_LIMIT),
        name="expert_mlp",
    )(blk_e, blk_cnt, tok3, tok3, u_tiles, wg, wu, wd, wg, wu, wd)


def _combine_kernel(final, pos_ref, nxt_ref, gate_ref, h_ref, p_ref, y_hbm, g_ref, wg_ref, wp_ref,
                    fg_ref, o_ref, ya, yb, sem):
    i = pl.program_id(0)
    last = pl.num_programs(0) - 1
    tm = h_ref.shape[0] // 2
    n = 2 * tm

    @pl.when(i == 0)
    def _():
        _issue_tiles(y_hbm, pos_ref, 0, ya, sem.at[0], n, False)

    def half(cur, cur_sem, nxt, nxt_sem, nxt_idx, nxt_off, rows):
        _wait_tiles(y_hbm, cur, cur_sem, n, True)
        _issue_tiles(y_hbm, nxt_idx, nxt_off, nxt, nxt_sem, n, True)
        gate = gate_ref[rows, :]
        h2 = (h_ref[rows, :] + gate[:, 0:1] * _load_token_tiles(cur, 0, tm)
              + gate[:, 1:2] * _load_token_tiles(cur, tm, tm))
        u = _rms(h2, g_ref[...]).astype(BF16)
        pgate = _sigmoid(jnp.dot(u, wg_ref[...], preferred_element_type=F32))
        pp = jnp.dot(p_ref[rows, :].astype(BF16), wp_ref[...], preferred_element_type=F32)
        h3 = h2 + pgate * pp
        if final:
            h3 = _rms(h3, fg_ref[...])
        o_ref[rows, :] = h3

    half(ya, sem.at[0], yb, sem.at[1], pos_ref, n, pl.ds(0, tm))
    half(yb, sem.at[1], ya, sem.at[0], nxt_ref, 0, pl.ds(tm, tm))

    @pl.when(i == last)
    def _():
        _wait_tiles(y_hbm, ya, sem.at[0], n, False)


def _combine_ple(final, pos, gate_w, h1, p, y_tiles, g, wg, wp, fg):
    t, d = h1.shape
    tm = min(256, t // 2)
    nb = t // (2 * tm)
    pos_blk = jnp.concatenate([pos[:, 0].reshape(2 * nb, 1, tm), pos[:, 1].reshape(2 * nb, 1, tm)], axis=2)
    pos_blk = pos_blk.reshape(nb, 1, 4 * tm)
    full = lambda shape: pl.BlockSpec(shape, lambda i: (0,) * len(shape))
    rows = lambda width: pl.BlockSpec((2 * tm, width), lambda i: (i, 0))
    return pl.pallas_call(
        functools.partial(_combine_kernel, final),
        grid=(nb,),
        in_specs=[
            pl.BlockSpec((None, 1, 4 * tm), lambda i: (i, 0, 0), memory_space=pltpu.SMEM),
            pl.BlockSpec((None, 1, 4 * tm), lambda i: (jnp.minimum(i + 1, nb - 1), 0, 0),
                         memory_space=pltpu.SMEM),
            rows(2), rows(d), rows(PLE_DIM),
            pl.BlockSpec(memory_space=pl.ANY),
            full((1, d)), full((d, d)), full((PLE_DIM, d)), full((1, d)),
        ],
        out_specs=rows(d),
        out_shape=jax.ShapeDtypeStruct((t, d), F32),
        scratch_shapes=[pltpu.VMEM((2 * tm * ROW_TILE, LANES), F32),
                        pltpu.VMEM((2 * tm * ROW_TILE, LANES), F32),
                        pltpu.SemaphoreType.DMA((2,))],
        compiler_params=pltpu.CompilerParams(
            dimension_semantics=("arbitrary",), vmem_limit_bytes=VMEM_LIMIT),
        name="combine_ple",
    )(pos_blk, pos_blk, gate_w, h1, p, y_tiles, g, wg, wp, fg)


def _route_plan(e_idx, local, counts):
    t = e_idx.shape[0]
    a = t * 2
    n_blocks = -(-a // MOE_BLOCK) + N_EXPERTS
    n_blocks += n_blocks % 2
    starts = jnp.cumsum(counts) - counts
    pcounts = (counts + MOE_BLOCK - 1) // MOE_BLOCK * MOE_BLOCK
    pend = jnp.cumsum(pcounts)
    pstarts = pend - pcounts
    onehot = e_idx[..., None] == jnp.arange(N_EXPERTS, dtype=jnp.int32)
    pos = local + jnp.sum(jnp.where(onehot, pstarts, 0), axis=-1)
    blk_start = jnp.arange(n_blocks, dtype=jnp.int32) * MOE_BLOCK
    blk_e = jnp.minimum(jnp.searchsorted(pend, blk_start, side='right'), N_EXPERTS - 1).astype(jnp.int32)
    blk_off = blk_start - pstarts[blk_e]
    blk_cnt = jnp.clip(counts[blk_e] - blk_off, 0, MOE_BLOCK).astype(jnp.int32)
    order = jnp.argsort(e_idx.reshape(-1)).astype(jnp.int32)
    r = jnp.arange(MOE_BLOCK, dtype=jnp.int32)[None, :]
    src = (starts[blk_e] + blk_off)[:, None] + r
    buf_tok = jnp.where(r < blk_cnt[:, None], order[jnp.clip(src, 0, a - 1)] // 2, 0).astype(jnp.int32)
    return blk_e, blk_cnt, buf_tok, pos.astype(jnp.int32)


def kernel(x, p, mix_norm, w_in, conv_w, conv_b, conv_ln_g, conv_ln_b, rwkv_mu, rwkv_w0, rwkv_w2, rwkv_a0, rwkv_a2, rwkv_g2, rwkv_kk, rwkv_ka, rwkv_rk, rwkv_gn_g, rwkv_gn_b, rwkv_v0, rwkv_v1, rwkv_v2, w_out, ffn_norm, router_group_w, router_group_b, router_expert_w, router_expert_b, expert_w_gate, expert_w_up, expert_w_down, ple_norm, ple_gate_w, ple_proj_w, final_norm):
    b, s, d = x.shape
    depth = w_in.shape[0]
    t = b * s
    n_lora = LORA_W + LORA_A + LORA_G
    row = lambda v: v.reshape(1, -1).astype(F32)

    h = x
    v_first = None
    for i in range(depth):
        w_conv = w_in[i, :, :2 * CONV_CH].astype(BF16)
        w_rwkv = jnp.pad(w_in[i, :, 2 * CONV_CH:], ((0, 0), (0, LORA_PAD - n_lora))).astype(BF16)
        mu = jnp.pad(rwkv_mu[i], (0, LORA_PAD - n_lora)).reshape(1, -1)
        cw = jnp.pad(conv_w[i], ((0, CONV_HALO - CONV_WIDTH), (0, 0)))
        yc = _conv_branch(h, row(mix_norm[i]), w_conv, cw, row(conv_b[i]), row(conv_ln_g[i]),
                          row(conv_ln_b[i]))

        wl = jnp.zeros((LORA_PAD, 3 * RWKV_CH), F32)
        wl = wl.at[0:LORA_W, 0:RWKV_CH].set(rwkv_w2[i])
        wl = wl.at[LORA_W:LORA_W + LORA_A, RWKV_CH:2 * RWKV_CH].set(rwkv_a2[i])
        wl = wl.at[LORA_W + LORA_A:n_lora, 2 * RWKV_CH:].set(rwkv_g2[i])
        if i == 0:
            vres = None
        else:
            v1 = jnp.pad(rwkv_v1[i - 1], ((0, 0), (0, LORA_V_PAD - rwkv_v1.shape[2]))).astype(BF16)
            v2 = jnp.pad(rwkv_v2[i - 1], ((0, LORA_V_PAD - rwkv_v2.shape[1]), (0, 0))).astype(BF16)
            vres = (v_first, row(rwkv_v0[i - 1]), v1, v2)
        res = _rwkv_branch(h, row(mix_norm[i]), w_rwkv, mu, row(rwkv_w0[i]), row(rwkv_a0[i]),
                           wl.astype(BF16), row(rwkv_kk[i]), row(rwkv_ka[i]), row(rwkv_rk[i]),
                           row(rwkv_gn_g[i]), row(rwkv_gn_b[i]), vres)
        if i == 0:
            yr, v_first = res
        else:
            yr = res

        rw = jnp.concatenate(
            [router_group_w[i]] + [router_expert_w[i, gidx] for gidx in range(N_GROUPS)], axis=1)
        rw = jnp.pad(rw, ((0, 0), (0, ROUTER_COLS - rw.shape[1])))
        rw_hi = rw.astype(BF16)
        rw_lo = (rw - rw_hi.astype(F32)).astype(BF16)
        rb = jnp.concatenate([router_group_b[i], router_expert_b[i].reshape(-1)])
        rb = jnp.pad(rb, (0, ROUTER_COLS - rb.shape[0])).reshape(1, -1)
        h1, u_tiles, e_idx, gate_w, local, counts = _out_router(
            h.reshape(t, d), yc.reshape(t, CONV_CH), yr.reshape(t, RWKV_CH),
            w_out[i].astype(BF16), row(ffn_norm[i]), rw_hi, rw_lo, rb)

        blk_e, blk_cnt, buf_tok, pos = _route_plan(e_idx, local, counts[0, :N_EXPERTS])
        y_tiles = _expert_mlp(blk_e, blk_cnt, buf_tok, u_tiles,
                              expert_w_gate[i].astype(BF16), expert_w_up[i].astype(BF16),
                              expert_w_down[i].astype(BF16))
        h = _combine_ple(i == depth - 1, pos, gate_w, h1, p[i].reshape(t, PLE_DIM), y_tiles,
                         row(ple_norm[i]), ple_gate_w[i].astype(BF16), ple_proj_w[i].astype(BF16),
                         row(final_norm)).reshape(b, s, d)
    return h
```
